```python
import math
import jax
import jax.numpy as jnp
from jax import lax
import numpy as np

D_MODEL = 2048
BATCH = 2
SEQ = 16384
DEPTH = 2

CHUNK = 64
Q_BLOCK = 128
D_MIX = D_MODEL
A_WIDTH = (3 * D_MIX) // 8
A_HEAD = 64
A_HEADS = A_WIDTH // A_HEAD
A_DECAY_LORA = 64
A_ICLR_LORA = 64
A_GATE_LORA = 128
A_IN = 3 * A_WIDTH + A_DECAY_LORA + A_ICLR_LORA + A_GATE_LORA
B_WIDTH = D_MIX // 4
B_GROUP = 16
B_GROUPS = B_WIDTH // B_GROUP
B_STATE = 64
C_WIDTH = D_MIX - A_WIDTH - B_WIDTH
C_HEAD = 64
C_VHEAD = 2 * C_HEAD
C_HEADS = C_WIDTH // C_VHEAD
C_IN = 3 * C_WIDTH
N_IN = A_IN + B_WIDTH + C_IN
D_FF = 4 * D_MODEL
RMS_EPS = 1e-6
SUBLN_EPS = 1e-5
RWKV_GN_EPS = 64e-5
NEG_INF = -1e30

kernel_name = 'hybrid_rwkv7_s5_diffattn_trunk'


def rms_norm(x, g, eps=RMS_EPS):
    xf = x.astype(jnp.float32)
    y = xf * lax.rsqrt(jnp.mean(xf * xf, axis=-1, keepdims=True) + eps)
    return (y * g.astype(jnp.float32)).astype(x.dtype)


def rwkv7_mixer(p, mu, w0, w_up, a0, a_up, g_up, k_k, k_a, r_k, gn_g, gn_b):
    f32 = jnp.float32
    p = p.astype(f32)
    bsz, seq, _ = p.shape
    prev = jnp.pad(p[:, :-1], ((0, 0), (1, 0), (0, 0)))
    xs = p + (prev - p) * mu.astype(f32)
    cuts = [A_WIDTH, 2 * A_WIDTH, 3 * A_WIDTH, 3 * A_WIDTH + A_DECAY_LORA,
            3 * A_WIDTH + A_DECAY_LORA + A_ICLR_LORA]
    r, k, v, lw, la, lg = jnp.split(xs, cuts, axis=-1)
    w = -jax.nn.softplus(-(w0.astype(f32) + jnp.tanh(lw) @ w_up.astype(f32))) - 0.5
    decay = jnp.exp(-jnp.exp(w))
    a = jax.nn.sigmoid(a0.astype(f32) + la @ a_up.astype(f32))
    g = jax.nn.sigmoid(lg) @ g_up.astype(f32)
    kk = k * k_k.astype(f32)
    k = k * (1.0 + (a - 1.0) * k_a.astype(f32))
    shp = (bsz, seq, A_HEADS, A_HEAD)
    r, k, v, kk, a, decay = (r.reshape(shp), k.reshape(shp), v.reshape(shp),
                             kk.reshape(shp), a.reshape(shp), decay.reshape(shp))
    kk = kk * lax.rsqrt(jnp.sum(kk * kk, axis=-1, keepdims=True) + 1e-12)

    def step(state, inp):
        r_t, w_t, k_t, v_t, kk_t, a_t = inp
        s_kk = jnp.einsum('bhvk,bhk->bhv', state, kk_t)
        state = (state * w_t[:, :, None, :]
                 - s_kk[..., None] * (kk_t * a_t)[:, :, None, :]
                 + v_t[..., None] * k_t[:, :, None, :])
        return state, jnp.einsum('bhvk,bhk->bhv', state, r_t)

    s0 = jnp.zeros((bsz, A_HEADS, A_HEAD, A_HEAD), f32)
    seq_in = (jnp.swapaxes(r, 0, 1), jnp.swapaxes(decay, 0, 1), jnp.swapaxes(k, 0, 1),
              jnp.swapaxes(v, 0, 1), jnp.swapaxes(kk, 0, 1), jnp.swapaxes(a, 0, 1))
    _, y = lax.scan(step, s0, seq_in)
    y = jnp.swapaxes(y, 0, 1)
    mean = jnp.mean(y, axis=-1, keepdims=True)
    var = jnp.mean(jnp.square(y - mean), axis=-1, keepdims=True)
    y = ((y - mean) * lax.rsqrt(var + RWKV_GN_EPS)).reshape(bsz, seq, A_WIDTH)
    y = y * gn_g.astype(f32) + gn_b.astype(f32)
    bonus = jnp.sum(r * k * r_k.astype(f32), axis=-1, keepdims=True) * v
    y = y + bonus.reshape(bsz, seq, A_WIDTH)
    return y * g


def s5_mixer(u, a_re, a_im, log_dt, b_re, b_im, c_re, c_im, d_skip, w_glu, b_glu, beta):
    f32 = jnp.float32
    bsz, seq, _ = u.shape
    u = u.astype(f32).reshape(bsz, seq, B_GROUPS, B_GROUP)
    ar, ai = a_re.astype(f32), a_im.astype(f32)
    dt = jnp.exp(log_dt.astype(f32))[:, None]
    mag = jnp.exp(dt * ar)
    abar_re, abar_im = mag * jnp.cos(dt * ai), mag * jnp.sin(dt * ai)
    den = ar * ar + ai * ai
    zr = ((abar_re - 1.0) * ar + abar_im * ai) / den
    zi = (abar_im * ar - (abar_re - 1.0) * ai) / den
    br, bi = b_re.astype(f32), b_im.astype(f32)
    bb_re = zr[..., None] * br - zi[..., None] * bi
    bb_im = zr[..., None] * bi + zi[..., None] * br
    bu_re = jnp.einsum('blgh,gph->blgp', u, bb_re)
    bu_im = jnp.einsum('blgh,gph->blgp', u, bb_im)
    a_el_re = jnp.broadcast_to(abar_re, (1, seq, B_GROUPS, B_STATE))
    a_el_im = jnp.broadcast_to(abar_im, (1, seq, B_GROUPS, B_STATE))

    def combine(e1, e2):
        a1r, a1i, b1r, b1i = e1
        a2r, a2i, b2r, b2i = e2
        return (a2r * a1r - a2i * a1i, a2r * a1i + a2i * a1r,
                a2r * b1r - a2i * b1i + b2r, a2r * b1i + a2i * b1r + b2i)

    _, _, xr, xi = lax.associative_scan(combine, (a_el_re, a_el_im, bu_re, bu_im), axis=1)
    y = (jnp.einsum('blgp,ghp->blgh', xr, c_re.astype(f32))
         - jnp.einsum('blgp,ghp->blgh', xi, c_im.astype(f32))
         + d_skip.astype(f32) * u)
    y = jax.nn.gelu(y)
    y = y * jax.nn.sigmoid(jnp.einsum('blgh,ghk->blgk', y, w_glu.astype(f32)) + b_glu.astype(f32))
    y = y * lax.rsqrt(jnp.mean(y * y, axis=-1, keepdims=True) + RMS_EPS)
    return y.reshape(bsz, seq, B_WIDTH) * beta.astype(f32)


def diff_attn_mixer(p, lq1, lk1, lq2, lk2, subln_g, lambda_init):
    f32 = jnp.float32
    p = p.astype(f32)
    bsz, seq, _ = p.shape
    q, k, v = jnp.split(p, [C_WIDTH, 2 * C_WIDTH], axis=-1)
    q = q.reshape(bsz, seq, C_HEADS, 2, C_HEAD)
    k = k.reshape(bsz, seq, C_HEADS, 2, C_HEAD)
    v = v.reshape(bsz, seq, C_HEADS, C_VHEAD)
    lam = (jnp.exp(jnp.sum(lq1.astype(f32) * lk1.astype(f32)))
           - jnp.exp(jnp.sum(lq2.astype(f32) * lk2.astype(f32))) + lambda_init)
    scale = C_HEAD ** -0.5
    n_blk = seq // Q_BLOCK
    qb = q.reshape(bsz, n_blk, Q_BLOCK, C_HEADS, 2, C_HEAD).transpose(1, 0, 2, 3, 4, 5)
    key_chunk = jnp.arange(seq) // CHUNK

    def block(args):
        q_blk, i = args
        q_chunk = (i * Q_BLOCK + jnp.arange(Q_BLOCK)) // CHUNK
        s = jnp.einsum('bqhmd,bkhmd->bhmqk', q_blk, k) * scale
        mask = key_chunk[None, :] <= q_chunk[:, None]
        pr = jax.nn.softmax(jnp.where(mask, s, NEG_INF), axis=-1)
        attn = pr[:, :, 0] - lam * pr[:, :, 1]
        return jnp.einsum('bhqk,bkhe->bqhe', attn, v)

    o = lax.map(block, (qb, jnp.arange(n_blk)))
    o = o.transpose(1, 0, 2, 3, 4).reshape(bsz, seq, C_HEADS, C_VHEAD)
    o = rms_norm(o, subln_g, SUBLN_EPS) * (1.0 - lambda_init)
    return o.reshape(bsz, seq, C_WIDTH)


def setup_inputs(seed: int = 0) -> dict:
    key = jax.random.key(seed)
    ks = iter(jax.random.split(key, 40))

    def nrm(shape, scale):
        return scale * jax.random.normal(next(ks), shape, jnp.float32)

    def unif(shape, lo, hi):
        return jax.random.uniform(next(ks), shape, jnp.float32, lo, hi)

    L, D = DEPTH, D_MODEL
    return {
        'x': nrm((BATCH, SEQ, D), 1.0),
        'norm_mix_g': 1.0 + nrm((L, D), 0.02),
        'w_in': nrm((L, D, N_IN), D ** -0.5),
        'rwkv_mu': unif((L, A_IN), 0.0, 1.0),
        'rwkv_w0': unif((L, A_WIDTH), -6.5, -1.5),
        'rwkv_w_up': nrm((L, A_DECAY_LORA, A_WIDTH), 0.5 * A_DECAY_LORA ** -0.5),
        'rwkv_a0': nrm((L, A_WIDTH), 0.1),
        'rwkv_a_up': nrm((L, A_ICLR_LORA, A_WIDTH), A_ICLR_LORA ** -0.5),
        'rwkv_g_up': nrm((L, A_GATE_LORA, A_WIDTH), A_GATE_LORA ** -0.5),
        'rwkv_k_k': 0.85 + nrm((L, A_WIDTH), 0.05),
        'rwkv_k_a': 1.0 + nrm((L, A_WIDTH), 0.05),
        'rwkv_r_k': nrm((L, A_HEADS, A_HEAD), 0.1),
        'rwkv_gn_g': 1.0 + nrm((L, A_WIDTH), 0.02),
        'rwkv_gn_b': nrm((L, A_WIDTH), 0.01),
        's5_a_re': -0.5 + nrm((L, B_GROUPS, B_STATE), 0.01),
        's5_a_im': jnp.pi * jnp.arange(B_STATE, dtype=jnp.float32) + nrm((L, B_GROUPS, B_STATE), 0.01),
        's5_log_dt': unif((L, B_GROUPS), math.log(1e-3), math.log(1e-1)),
        's5_b_re': nrm((L, B_GROUPS, B_STATE, B_GROUP), (2 * B_GROUP) ** -0.5),
        's5_b_im': nrm((L, B_GROUPS, B_STATE, B_GROUP), (2 * B_GROUP) ** -0.5),
        's5_c_re': nrm((L, B_GROUPS, B_GROUP, B_STATE), (2 * B_STATE) ** -0.5),
        's5_c_im': nrm((L, B_GROUPS, B_GROUP, B_STATE), (2 * B_STATE) ** -0.5),
        's5_d': nrm((L, B_GROUPS, B_GROUP), 1.0),
        's5_w_glu': nrm((L, B_GROUPS, B_GROUP, B_GROUP), B_GROUP ** -0.5),
        's5_b_glu': nrm((L, B_GROUPS, B_GROUP), 0.01),
        's5_beta': 1.0 + nrm((L, B_WIDTH), 0.02),
        'diff_lq1': nrm((L, C_HEAD), 0.1),
        'diff_lk1': nrm((L, C_HEAD), 0.1),
        'diff_lq2': nrm((L, C_HEAD), 0.1),
        'diff_lk2': nrm((L, C_HEAD), 0.1),
        'diff_subln_g': 1.0 + nrm((L, C_HEADS, C_VHEAD), 0.02),
        'w_out': nrm((L, D_MIX, D), D_MIX ** -0.5),
        'norm_ff_g': 1.0 + nrm((L, D), 0.02),
        'w_ff1': nrm((L, D, D_FF), D ** -0.5),
        'w_ff2': nrm((L, D_FF, D), D_FF ** -0.5),
        'norm_final_g': 1.0 + nrm((D,), 0.02),
    }


def reference(x, norm_mix_g, w_in, rwkv_mu, rwkv_w0, rwkv_w_up, rwkv_a0, rwkv_a_up, rwkv_g_up,
              rwkv_k_k, rwkv_k_a, rwkv_r_k, rwkv_gn_g, rwkv_gn_b, s5_a_re, s5_a_im, s5_log_dt,
              s5_b_re, s5_b_im, s5_c_re, s5_c_im, s5_d, s5_w_glu, s5_b_glu, s5_beta,
              diff_lq1, diff_lk1, diff_lq2, diff_lk2, diff_subln_g, w_out, norm_ff_g,
              w_ff1, w_ff2, norm_final_g):
    for l in range(DEPTH):
        h = rms_norm(x, norm_mix_g[l])
        proj = h @ w_in[l]
        p_a, p_b, p_c = jnp.split(proj, [A_IN, A_IN + B_WIDTH], axis=-1)
        y_a = rwkv7_mixer(p_a, rwkv_mu[l], rwkv_w0[l], rwkv_w_up[l], rwkv_a0[l], rwkv_a_up[l],
                          rwkv_g_up[l], rwkv_k_k[l], rwkv_k_a[l], rwkv_r_k[l],
                          rwkv_gn_g[l], rwkv_gn_b[l])
        y_b = s5_mixer(p_b, s5_a_re[l], s5_a_im[l], s5_log_dt[l], s5_b_re[l], s5_b_im[l],
                       s5_c_re[l], s5_c_im[l], s5_d[l], s5_w_glu[l], s5_b_glu[l], s5_beta[l])
        lambda_init = 0.8 - 0.6 * math.exp(-0.3 * l)
        y_c = diff_attn_mixer(p_c, diff_lq1[l], diff_lk1[l], diff_lq2[l], diff_lk2[l],
                              diff_subln_g[l], lambda_init)
        mixed = jnp.concatenate([y_a.astype(x.dtype), y_b.astype(x.dtype), y_c.astype(x.dtype)], axis=-1)
        x = x + mixed @ w_out[l]
        h = rms_norm(x, norm_ff_g[l])
        x = x + jnp.square(jax.nn.relu(h @ w_ff1[l])) @ w_ff2[l]
    return rms_norm(x, norm_final_g)
```

```python
import functools
import math

import jax
import jax.numpy as jnp
import numpy as np
from jax import lax
from jax.experimental import pallas as pl
from jax.experimental.pallas import tpu as pltpu

F32 = jnp.float32
BF16 = jnp.bfloat16

D_MODEL = 2048
DEPTH = 2
CHUNK = 64
CHUNK_SHIFT = 6
A_WIDTH = 768
A_HEAD = 64
A_PAIRS = A_WIDTH // (2 * A_HEAD)
A_DECAY_LORA = 64
A_ICLR_LORA = 64
A_GATE_LORA = 128
A_IN = 3 * A_WIDTH + A_DECAY_LORA + A_ICLR_LORA + A_GATE_LORA
B_WIDTH = 512
B_GROUP = 16
B_GROUPS = B_WIDTH // B_GROUP
B_STATE = 64
C_WIDTH = 768
C_HEAD = 64
C_VHEAD = 2 * C_HEAD
C_HEADS = C_WIDTH // C_VHEAD
C_IN = 3 * C_WIDTH
N_IN = A_IN + B_WIDTH + C_IN
D_FF = 4 * D_MODEL
RMS_EPS = 1e-6
SUBLN_EPS = 1e-5
RWKV_GN_EPS = 64e-5
NEG_INF = -1e30

LANES = 128
VMEM_LIMIT = 56 * 1024 * 1024
S5_LC = 64
S5_ROW = S5_LC * B_GROUP


def _dot(a, b):
    return jnp.dot(a, b, preferred_element_type=F32)


def _dot_nt(a, b):
    return lax.dot_general(a, b, (((1,), (1,)), ((), ())), preferred_element_type=F32)


def _split3(x):
    hi = x.astype(BF16)
    r1 = x - hi.astype(F32)
    mid = r1.astype(BF16)
    lo = (r1 - mid.astype(F32)).astype(BF16)
    return hi, mid, lo


def _dot_x3(x, m):
    hi, mid, lo = _split3(x)
    return _dot(hi, m) + _dot(mid, m) + _dot(lo, m)


def _dot_x3_left(m, x):
    hi, mid, lo = _split3(x)
    return _dot(m, hi) + _dot(m, mid) + _dot(m, lo)


def _dot_hl(x, m_hi, m_lo):
    hi, mid, _ = _split3(x)
    return _dot(hi, m_hi) + _dot(mid, m_hi) + _dot(hi, m_lo)


def _rms(x, g, eps):
    return x * lax.rsqrt(jnp.mean(x * x, axis=-1, keepdims=True) + eps) * g


def _inproj_kernel(x_ref, g_ref, w_ref, o_ref, h_ref):
    @pl.when(pl.program_id(1) == 0)
    def _():
        h_ref[...] = _rms(x_ref[...], g_ref[...], RMS_EPS).astype(BF16)

    o_ref[...] = _dot(h_ref[...], w_ref[...])


def _inproj(x2, g, w_bf, tm, tn):
    t, d = x2.shape
    n = w_bf.shape[1]
    return pl.pallas_call(
        _inproj_kernel,
        grid=(t // tm, n // tn),
        in_specs=[pl.BlockSpec((tm, d), lambda i, j: (i, 0)),
                  pl.BlockSpec((1, d), lambda i, j: (0, 0)),
                  pl.BlockSpec((d, tn), lambda i, j: (0, j))],
        out_specs=pl.BlockSpec((tm, tn), lambda i, j: (i, j)),
        out_shape=jax.ShapeDtypeStruct((t, n), F32),
        scratch_shapes=[pltpu.VMEM((tm, d), BF16)],
        compiler_params=pltpu.CompilerParams(
            dimension_semantics=("arbitrary", "arbitrary"), vmem_limit_bytes=VMEM_LIMIT),
        name="inproj",
    )(x2, g.reshape(1, d), w_bf)


def _outproj_kernel(x_ref, ya_ref, yb_ref, yc_ref, wa_ref, wb_ref, wc_ref, o_ref):
    acc = _dot(ya_ref[...].astype(BF16), wa_ref[...])
    acc += _dot(yb_ref[...].astype(BF16), wb_ref[...])
    acc += _dot(yc_ref[...].astype(BF16), wc_ref[...])
    o_ref[...] = x_ref[...] + acc


def _outproj(x2, ya, yb, yc, w_bf, tm):
    t, d = x2.shape
    wa, wb, wc = w_bf[:A_WIDTH], w_bf[A_WIDTH:A_WIDTH + B_WIDTH], w_bf[A_WIDTH + B_WIDTH:]
    row = lambda w: pl.BlockSpec((tm, w), lambda i: (i, 0))
    full = lambda a: pl.BlockSpec(a.shape, lambda i: (0, 0))
    return pl.pallas_call(
        _outproj_kernel,
        grid=(t // tm,),
        in_specs=[row(d), row(A_WIDTH), row(B_WIDTH), row(C_WIDTH), full(wa), full(wb), full(wc)],
        out_specs=row(d),
        out_shape=jax.ShapeDtypeStruct((t, d), F32),
        compiler_params=pltpu.CompilerParams(
            dimension_semantics=("arbitrary",), vmem_limit_bytes=VMEM_LIMIT),
        name="outproj",
    )(x2, ya, yb, yc, wa, wb, wc)


def _ffn_kernel(x_ref, g_ref, w1_ref, w2_ref, gf_ref, o_ref, h_ref, acc_ref, *, final_norm):
    f = pl.program_id(1)

    @pl.when(f == 0)
    def _():
        h_ref[...] = _rms(x_ref[...], g_ref[...], RMS_EPS).astype(BF16)
        acc_ref[...] = jnp.zeros_like(acc_ref)

    a = _dot(h_ref[...], w1_ref[...])
    a = jnp.square(jnp.maximum(a, 0.0))
    acc_ref[...] += _dot(a.astype(BF16), w2_ref[...])

    @pl.when(f == pl.num_programs(1) - 1)
    def _():
        y = x_ref[...] + acc_ref[...]
        if final_norm:
            y = _rms(y, gf_ref[...], RMS_EPS)
        o_ref[...] = y


def _ffn(x2, g, w1_bf, w2_bf, gf, final_norm, tm, tf):
    t, d = x2.shape
    dff = w1_bf.shape[1]
    return pl.pallas_call(
        functools.partial(_ffn_kernel, final_norm=final_norm),
        grid=(t // tm, dff // tf),
        in_specs=[pl.BlockSpec((tm, d), lambda i, j: (i, 0)),
                  pl.BlockSpec((1, d), lambda i, j: (0, 0)),
                  pl.BlockSpec((d, tf), lambda i, j: (0, j)),
                  pl.BlockSpec((tf, d), lambda i, j: (j, 0)),
                  pl.BlockSpec((1, d), lambda i, j: (0, 0))],
        out_specs=pl.BlockSpec((tm, d), lambda i, j: (i, 0)),
        out_shape=jax.ShapeDtypeStruct((t, d), F32),
        scratch_shapes=[pltpu.VMEM((tm, d), BF16), pltpu.VMEM((tm, d), F32)],
        compiler_params=pltpu.CompilerParams(
            dimension_semantics=("arbitrary", "arbitrary"), vmem_limit_bytes=VMEM_LIMIT),
        name="ffn",
    )(x2, g.reshape(1, d), w1_bf, w2_bf, gf.reshape(1, d))


def _attn_kernel(qt_ref, kt_ref, q_ref, k_ref, v_ref, lam_ref, g_ref, o_ref,
                 qs_ref, m_ref, l_ref, acc_ref, *, tq, out_scale):
    s_id = pl.program_id(2)
    qi = qt_ref[s_id]
    kj = kt_ref[s_id]

    @pl.when(kj == 0)
    def _():
        q = q_ref[...] * (C_HEAD ** -0.5)
        first = lax.broadcasted_iota(jnp.int32, q.shape, 1) < C_HEAD
        qs_ref[0:tq, :] = jnp.where(first, q, 0.0).astype(BF16)
        qs_ref[tq:, :] = jnp.where(first, 0.0, q).astype(BF16)
        m_ref[...] = jnp.full_like(m_ref, NEG_INF)
        l_ref[...] = jnp.zeros_like(l_ref)
        acc_ref[...] = jnp.zeros_like(acc_ref)

    def step(masked):
        s = _dot_nt(qs_ref[...], k_ref[...].astype(BF16))
        if masked:
            qpos = lax.broadcasted_iota(jnp.int32, s.shape, 0) & (tq - 1)
            kpos = lax.broadcasted_iota(jnp.int32, s.shape, 1)
            s = jnp.where((kpos >> CHUNK_SHIFT) <= (qpos >> CHUNK_SHIFT), s, NEG_INF)
        m_prev = m_ref[...]
        m_new = jnp.maximum(m_prev, jnp.max(s, axis=-1, keepdims=True))
        alpha = jnp.exp(m_prev - m_new)
        p = jnp.exp(s - m_new)
        l_ref[...] = alpha * l_ref[...] + jnp.sum(p, axis=-1, keepdims=True)
        acc_ref[...] = alpha * acc_ref[...] + _dot(p.astype(BF16), v_ref[...].astype(BF16))
        m_ref[...] = m_new

    @pl.when(kj < qi)
    def _():
        step(False)

    @pl.when(kj == qi)
    def _():
        step(True)
        o = acc_ref[...] / l_ref[...]
        o = o[0:tq] - lam_ref[...] * o[tq:]
        o_ref[...] = _rms(o, g_ref[...], SUBLN_EPS) * out_scale


def _attn(proj, lam, subln_g, bsz, seq, lambda_init, tq):
    t = proj.shape[0]
    nq = seq // tq
    pairs = [(i, j) for i in range(nq) for j in range(i + 1)]
    qt = jnp.asarray([p[0] for p in pairs], jnp.int32)
    kt = jnp.asarray([p[1] for p in pairs], jnp.int32)
    qoff = (A_IN + B_WIDTH) // LANES
    koff = qoff + C_WIDTH // LANES
    voff = koff + C_WIDTH // LANES
    blk = lambda off, tab: pl.BlockSpec(
        (tq, LANES), lambda b, h, s, qt_, kt_: (b * nq + (qt_ if tab == 0 else kt_)[s], off + h))
    grid_spec = pltpu.PrefetchScalarGridSpec(
        num_scalar_prefetch=2,
        grid=(bsz, C_HEADS, len(pairs)),
        in_specs=[blk(qoff, 0), blk(koff, 1), blk(voff, 1),
                  pl.BlockSpec((1, LANES), lambda b, h, s, qt_, kt_: (0, 0)),
                  pl.BlockSpec((None, 1, LANES), lambda b, h, s, qt_, kt_: (h, 0, 0))],
        out_specs=pl.BlockSpec((tq, LANES), lambda b, h, s, qt_, kt_: (b * nq + qt_[s], h)),
        scratch_shapes=[pltpu.VMEM((2 * tq, LANES), BF16), pltpu.VMEM((2 * tq, 1), F32),
                        pltpu.VMEM((2 * tq, 1), F32), pltpu.VMEM((2 * tq, LANES), F32)])
    return pl.pallas_call(
        functools.partial(_attn_kernel, tq=tq, out_scale=1.0 - lambda_init),
        grid_spec=grid_spec,
        out_shape=jax.ShapeDtypeStruct((t, C_WIDTH), F32),
        compiler_params=pltpu.CompilerParams(
            dimension_semantics=("arbitrary", "arbitrary", "arbitrary"), vmem_limit_bytes=VMEM_LIMIT),
        name="diffattn",
    )(qt, kt, proj, proj, proj, jnp.broadcast_to(lam.reshape(1, 1), (1, LANES)).astype(F32),
      subln_g.reshape(C_HEADS, 1, C_VHEAD))


def _s5_params(a_re, a_im, log_dt, b_re, b_im, c_re, c_im, d_skip, w_glu, b_glu, beta, n_levels):
    hp = lax.Precision.HIGHEST
    g, p = a_re.shape
    dt = jnp.exp(log_dt)[:, None]
    mag = jnp.exp(dt * a_re)
    abr, abi = mag * jnp.cos(dt * a_im), mag * jnp.sin(dt * a_im)
    den = a_re * a_re + a_im * a_im
    zr = ((abr - 1.0) * a_re + abi * a_im) / den
    zi = (abi * a_re - (abr - 1.0) * a_im) / den
    bbr = zr[..., None] * b_re - zi[..., None] * b_im
    bbi = zr[..., None] * b_im + zi[..., None] * b_re
    pwr, pwi = jnp.ones((g, 1, p), F32), jnp.zeros((g, 1, p), F32)
    qr, qi = abr[:, None, :], abi[:, None, :]
    n = 1
    while n < S5_LC:
        pwr, pwi = (jnp.concatenate([pwr, pwr * qr - pwi * qi], 1),
                    jnp.concatenate([pwi, pwr * qi + pwi * qr], 1))
        qr, qi = qr * qr - qi * qi, 2.0 * qr * qi
        n *= 2
    lev1, lev2 = [], []
    for _ in range(n_levels):
        lev1.append(jnp.concatenate([qr, qr], -1))
        lev2.append(jnp.concatenate([-qi, qi], -1))
        qr, qi = qr * qr - qi * qi, 2.0 * qr * qi
    lev1 = jnp.concatenate(lev1, 1)
    lev2 = jnp.concatenate(lev2, 1)
    pw1r = jnp.concatenate([pwr[:, 1:], lev1[:, :1, :p]], 1)
    pw1i = jnp.concatenate([pwi[:, 1:], lev2[:, :1, p:]], 1)
    cpr = c_re[:, None] * pwr[:, :, None, :] - c_im[:, None] * pwi[:, :, None, :]
    cpi = c_re[:, None] * pwi[:, :, None, :] + c_im[:, None] * pwr[:, :, None, :]
    kj = (jnp.einsum('gjhp,gpk->gjhk', cpr, bbr, precision=hp)
          - jnp.einsum('gjhp,gpk->gjhk', cpi, bbi, precision=hp))
    kj = kj.at[:, 0].add(d_skip[:, :, None] * jnp.eye(B_GROUP, dtype=F32))
    lag = jnp.arange(S5_LC)[None, :] - jnp.arange(S5_LC)[:, None]
    kt = kj[:, jnp.clip(lag, 0, S5_LC - 1)] * (lag >= 0)[None, :, :, None, None]
    kmat = kt.transpose(0, 1, 4, 2, 3).reshape(g, S5_ROW, S5_ROW)
    rr, ri = pwr[:, ::-1], pwi[:, ::-1]
    bpr = rr[:, :, :, None] * bbr[:, None] - ri[:, :, :, None] * bbi[:, None]
    bpi = rr[:, :, :, None] * bbi[:, None] + ri[:, :, :, None] * bbr[:, None]
    bmat = jnp.concatenate([bpr.transpose(0, 1, 3, 2), bpi.transpose(0, 1, 3, 2)], -1)
    bmat = bmat.reshape(g, S5_ROW, 2 * p)
    cqr = c_re[:, None] * pw1r[:, :, None, :] - c_im[:, None] * pw1i[:, :, None, :]
    cqi = c_re[:, None] * pw1i[:, :, None, :] + c_im[:, None] * pw1r[:, :, None, :]
    cmat = jnp.concatenate([cqr.transpose(0, 3, 1, 2), -cqi.transpose(0, 3, 1, 2)], 1)
    cmat = cmat.reshape(g, 2 * p, S5_ROW)
    rep = LANES // B_GROUP
    eye = jnp.eye(rep, dtype=F32)
    glu = jnp.einsum('ab,ghk->gahbk', eye, w_glu).reshape(g, LANES, LANES)
    bglu = jnp.tile(b_glu, (1, rep)).reshape(g, 1, LANES)
    betat = jnp.tile(beta.reshape(g, B_GROUP), (1, S5_LC)).reshape(g, 1, S5_ROW)
    return (kmat.astype(BF16), bmat.astype(BF16), cmat.astype(BF16), glu.astype(BF16),
            bglu, betat, lev1, lev2)


def _gelu_tanh(x):
    return 0.5 * x * (1.0 + jnp.tanh(math.sqrt(2.0 / math.pi) * (x + 0.044715 * (x * x * x))))


def _s5_kernel(u_ref, kmat_ref, bmat_ref, cmat_ref, glu_ref, bglu_ref, beta_ref, lev1_ref, lev2_ref,
               mean_ref, o_ref, *, rows_per_batch, n_levels):
    u = u_ref[...].astype(BF16)
    y = _dot(u, kmat_ref[...])
    x = _dot(u, bmat_ref[...])
    rpos = lax.broadcasted_iota(jnp.int32, x.shape, 0) & (rows_per_batch - 1)
    for lv in range(n_levels):
        d = 1 << lv
        if d >= rows_per_batch:
            break
        sh = jnp.where(rpos >= d, pltpu.roll(x, d, axis=0), 0.0)
        sw = pltpu.roll(sh, B_STATE, axis=1)
        x = x + lev1_ref[lv:lv + 1, :] * sh + lev2_ref[lv:lv + 1, :] * sw
    xs = jnp.where(rpos >= 1, pltpu.roll(x, 1, axis=0), 0.0)
    y = y + _dot(xs.astype(BF16), cmat_ref[...])
    y = _gelu_tanh(y)
    glu = glu_ref[...]
    mean = mean_ref[...]
    for c in range(S5_ROW // LANES):
        sl = slice(c * LANES, (c + 1) * LANES)
        yc = y[:, sl]
        z = _dot(yc.astype(BF16), glu) + bglu_ref[...]
        yc = yc * jax.nn.sigmoid(z)
        ms = _dot_x3(yc * yc, mean)
        o_ref[:, sl] = yc * lax.rsqrt(ms + RMS_EPS) * beta_ref[:, sl]


def _s5(proj, params, bsz, seq):
    t = proj.shape[0]
    rows = t // S5_LC
    rows_per_batch = seq // S5_LC
    n_levels = max(1, (rows_per_batch - 1).bit_length())
    kmat, bmat, cmat, glu, bglu, betat, lev1, lev2 = params
    u = proj[:, A_IN:A_IN + B_WIDTH].reshape(rows, S5_LC, B_GROUPS, B_GROUP)
    u = u.transpose(2, 0, 1, 3).reshape(B_GROUPS, rows, S5_ROW)
    rep = LANES // B_GROUP
    mean = jnp.kron(jnp.eye(rep, dtype=F32), jnp.full((B_GROUP, B_GROUP), 1.0 / B_GROUP, F32)).astype(BF16)
    per_g = lambda a: pl.BlockSpec((None,) + a.shape[1:], lambda g: (g,) + (0,) * (a.ndim - 1))
    y = pl.pallas_call(
        functools.partial(_s5_kernel, rows_per_batch=rows_per_batch, n_levels=n_levels),
        grid=(B_GROUPS,),
        in_specs=[per_g(u), per_g(kmat), per_g(bmat), per_g(cmat), per_g(glu), per_g(bglu), per_g(betat),
                  per_g(lev1), per_g(lev2), pl.BlockSpec(mean.shape, lambda g: (0, 0))],
        out_specs=pl.BlockSpec((None, rows, S5_ROW), lambda g: (g, 0, 0)),
        out_shape=jax.ShapeDtypeStruct((B_GROUPS, rows, S5_ROW), F32),
        compiler_params=pltpu.CompilerParams(
            dimension_semantics=("arbitrary",), vmem_limit_bytes=VMEM_LIMIT),
        name="s5",
    )(u, kmat, bmat, cmat, glu, bglu, betat, lev1, lev2, mean)
    return y.reshape(B_GROUPS, rows, S5_LC, B_GROUP).transpose(1, 2, 0, 3).reshape(t, B_WIDTH)


def _rwkv_chunk_pair(r, lw_cum, lw, k, v, kk, a, s0, strict, incl, first):
    c = r.shape[0]
    g = jnp.exp(lw_cum)
    gp = jnp.exp(lw_cum - lw)
    gi = jnp.exp(-lw_cum)
    g_end = g[c - 1:c, :]
    al = -kk * gp
    be = kk * a * gi
    kh = k * gi
    rt = r * g
    st = lambda x: jnp.concatenate([jnp.where(first, x, 0.0), jnp.where(first, 0.0, x)], axis=0).astype(BF16)
    dup = lambda x: jnp.concatenate([x, x], axis=0).astype(BF16)
    be_s, kh_s, v_s = st(be), st(kh), st(v)
    gmat = _dot_nt(jnp.concatenate([dup(al), dup(rt)], 0), jnp.concatenate([be_s, kh_s], 0))
    a_ab = jnp.where(strict, gmat[:2 * c, :2 * c], 0.0).astype(BF16)
    a_ak = jnp.where(strict, gmat[:2 * c, 2 * c:], 0.0).astype(BF16)
    a_rb = jnp.where(incl, gmat[2 * c:, :2 * c], 0.0).astype(BF16)
    a_rk = jnp.where(incl, gmat[2 * c:, 2 * c:], 0.0).astype(BF16)
    ls = _dot_nt(jnp.concatenate([st(al), st(rt)], 0), s0.astype(BF16))
    av = _dot(jnp.concatenate([a_ak, a_rk], 0), v_s)
    pows = [a_ab]
    n = 1
    while 2 * n < c:
        pows.append(_dot(pows[-1], pows[-1]).astype(BF16))
        n *= 2
    u = ls[:2 * c] + av[:2 * c]
    for pm in reversed(pows):
        u = u + _dot(pm, u.astype(BF16))
    u_bf = u.astype(BF16)
    yst = ls[2 * c:] + av[2 * c:] + _dot(a_rb, u_bf)
    y = yst[:c] + yst[c:]
    lhs_t = jnp.concatenate([u, v_s.astype(F32)], 0).T.astype(BF16)
    rhs = jnp.concatenate([st(be * g_end), st(kh * g_end)], 0)
    s_new = s0 * g_end + _dot(lhs_t, rhs)
    return y, s_new


def _rwkv_kernel(p_ref, mu_ref, w0_ref, a0_ref, wch_ref, wcl_ref, guh_ref, gul_ref, kkp_ref, kap_ref,
                 rk_ref, gng_ref, gnb_ref, seg_ref, tri_ref, o_ref,
                 state_ref, carry_ref, r_s, lw_s, cum_s, k_s, v_s, kk_s, a_s, y_s, *, tb):
    ti = pl.program_id(1)

    @pl.when(ti == 0)
    def _():
        state_ref[...] = jnp.zeros_like(state_ref)
        carry_ref[...] = jnp.zeros_like(carry_ref)

    p = p_ref[...]
    row = lax.broadcasted_iota(jnp.int32, p.shape, 0)
    prev = jnp.where(row == 0, carry_ref[0:1, :], pltpu.roll(p, 1, axis=0))
    carry_ref[0:1, :] = p[tb - 1:tb, :]
    xs = p + (prev - p) * mu_ref[...]
    r = xs[:, 0:A_WIDTH]
    k = xs[:, A_WIDTH:2 * A_WIDTH]
    v = xs[:, 2 * A_WIDTH:3 * A_WIDTH]
    z = xs[:, 3 * A_WIDTH:3 * A_WIDTH + LANES]
    lane = lax.broadcasted_iota(jnp.int32, z.shape, 1)
    z = jnp.where(lane < A_DECAY_LORA, jnp.tanh(z), z)
    wa = _dot_hl(z, wch_ref[...], wcl_ref[...])
    w = -jax.nn.softplus(-(w0_ref[...] + wa[:, :A_WIDTH])) - 0.5
    lw = -jnp.exp(w)
    a = jax.nn.sigmoid(a0_ref[...] + wa[:, A_WIDTH:])
    gate = _dot_hl(jax.nn.sigmoid(xs[:, 3 * A_WIDTH + LANES:]), guh_ref[...], gul_ref[...])
    kk = k * kkp_ref[...]
    k = k * (1.0 + (a - 1.0) * kap_ref[...])
    seg = seg_ref[...]
    blocks = [slice(j * LANES, (j + 1) * LANES) for j in range(A_PAIRS)]
    segsum = lambda x: jnp.concatenate([_dot_x3(x[:, b], seg) for b in blocks], axis=1)
    kk = kk * lax.rsqrt(segsum(kk * kk) + 1e-12)
    r_s[...] = r
    lw_s[...] = lw
    k_s[...] = k
    v_s[...] = v
    kk_s[...] = kk
    a_s[...] = a
    tri = tri_ref[...]
    for c in range(tb // CHUNK):
        rows = slice(c * CHUNK, (c + 1) * CHUNK)
        cum_s[rows, :] = _dot_x3_left(tri, lw[rows, :])

    i2 = lax.broadcasted_iota(jnp.int32, (2 * CHUNK, 2 * CHUNK), 0)
    j2 = lax.broadcasted_iota(jnp.int32, (2 * CHUNK, 2 * CHUNK), 1)
    same = (i2 >> CHUNK_SHIFT) == (j2 >> CHUNK_SHIFT)
    strict = same & ((j2 & (CHUNK - 1)) < (i2 & (CHUNK - 1)))
    incl = same & ((j2 & (CHUNK - 1)) <= (i2 & (CHUNK - 1)))
    first = lax.broadcasted_iota(jnp.int32, (CHUNK, LANES), 1) < A_HEAD

    def chunk_body(c, carry):
        rows = pl.ds(pl.multiple_of(c * CHUNK, CHUNK), CHUNK)
        for j, b in enumerate(blocks):
            y, s_new = _rwkv_chunk_pair(r_s[rows, b], cum_s[rows, b], lw_s[rows, b], k_s[rows, b],
                                        v_s[rows, b], kk_s[rows, b], a_s[rows, b], state_ref[j],
                                        strict, incl, first)
            y_s[rows, b] = y
            state_ref[j] = s_new
        return carry

    lax.fori_loop(0, tb // CHUNK, chunk_body, 0)

    y = y_s[...]
    inv = 1.0 / A_HEAD
    mean = segsum(y) * inv
    yc = y - mean
    var = segsum(yc * yc) * inv
    y = yc * lax.rsqrt(var + RWKV_GN_EPS) * gng_ref[...] + gnb_ref[...]
    bonus = segsum(r * k * rk_ref[...]) * v
    o_ref[...] = (y + bonus) * gate


def _rwkv(proj, prm, bsz, seq, tb):
    t = proj.shape[0]
    nt = seq // tb
    (mu, w0, a0, wch, wcl, guh, gul, kkp, kap, rk, gng, gnb) = prm
    seg = jnp.kron(jnp.eye(2, dtype=F32), jnp.ones((A_HEAD, A_HEAD), F32)).astype(BF16)
    tri = jnp.tril(jnp.ones((CHUNK, CHUNK), F32)).astype(BF16)
    full = lambda a_: pl.BlockSpec(a_.shape, lambda b, i: (0,) * a_.ndim)
    vm = lambda: pltpu.VMEM((tb, A_WIDTH), F32)
    return pl.pallas_call(
        functools.partial(_rwkv_kernel, tb=tb),
        grid=(bsz, nt),
        in_specs=[pl.BlockSpec((tb, A_IN), lambda b, i: (b * nt + i, 0))]
        + [full(x) for x in (mu, w0, a0, wch, wcl, guh, gul, kkp, kap, rk, gng, gnb, seg, tri)],
        out_specs=pl.BlockSpec((tb, A_WIDTH), lambda b, i: (b * nt + i, 0)),
        out_shape=jax.ShapeDtypeStruct((t, A_WIDTH), F32),
        scratch_shapes=[pltpu.VMEM((A_PAIRS, LANES, LANES), F32), pltpu.VMEM((8, A_IN), F32)]
        + [vm() for _ in range(8)],
        compiler_params=pltpu.CompilerParams(
            dimension_semantics=("arbitrary", "arbitrary"), vmem_limit_bytes=VMEM_LIMIT),
        name="rwkv7",
    )(proj, mu, w0, a0, wch, wcl, guh, gul, kkp, kap, rk, gng, gnb, seg, tri)


def _rwkv_params(mu, w0, w_up, a0, a_up, g_up, k_k, k_a, r_k, gn_g, gn_b):
    row = lambda x: x.reshape(1, -1).astype(F32)
    zeros = jnp.zeros_like(w_up)
    wc = jnp.concatenate([jnp.concatenate([w_up, zeros], 1), jnp.concatenate([zeros, a_up], 1)], 0)
    hl = lambda m: (m.astype(BF16), (m - m.astype(BF16).astype(F32)).astype(BF16))
    wch, wcl = hl(wc)
    guh, gul = hl(g_up)
    return (row(mu), row(w0), row(a0), wch, wcl, guh, gul, row(k_k), row(k_a), row(r_k), row(gn_g),
            row(gn_b))


def _pick(n, pref):
    t = min(n, pref)
    while n % t:
        t //= 2
    return t


def kernel(x, norm_mix_g, w_in, rwkv_mu, rwkv_w0, rwkv_w_up, rwkv_a0, rwkv_a_up, rwkv_g_up, rwkv_k_k, rwkv_k_a, rwkv_r_k, rwkv_gn_g, rwkv_gn_b, s5_a_re, s5_a_im, s5_log_dt, s5_b_re, s5_b_im, s5_c_re, s5_c_im, s5_d, s5_w_glu, s5_b_glu, s5_beta, diff_lq1, diff_lk1, diff_lq2, diff_lk2, diff_subln_g, w_out, norm_ff_g, w_ff1, w_ff2, norm_final_g):
    bsz, seq, d = x.shape
    t = bsz * seq
    depth = w_in.shape[0]
    x2 = x.reshape(t, d).astype(F32)
    tm = _pick(t, 512)
    tq = _pick(seq, 512)
    tb = _pick(seq, 256)
    assert tq & (tq - 1) == 0 and (seq // S5_LC) & (seq // S5_LC - 1) == 0 and tb % CHUNK == 0
    rows_per_batch = seq // S5_LC
    n_levels = max(1, (rows_per_batch - 1).bit_length())
    for l in range(depth):
        proj = _inproj(x2, norm_mix_g[l], w_in[l].astype(BF16), tm, 768)
        prm = _rwkv_params(rwkv_mu[l], rwkv_w0[l], rwkv_w_up[l], rwkv_a0[l], rwkv_a_up[l], rwkv_g_up[l],
                           rwkv_k_k[l], rwkv_k_a[l], rwkv_r_k[l].reshape(-1), rwkv_gn_g[l], rwkv_gn_b[l])
        y_a = _rwkv(proj, prm, bsz, seq, tb)
        s5p = _s5_params(s5_a_re[l], s5_a_im[l], s5_log_dt[l], s5_b_re[l], s5_b_im[l], s5_c_re[l],
                         s5_c_im[l], s5_d[l], s5_w_glu[l], s5_b_glu[l], s5_beta[l], n_levels)
        y_b = _s5(proj, s5p, bsz, seq)
        lambda_init = 0.8 - 0.6 * math.exp(-0.3 * l)
        lam = (jnp.exp(jnp.sum(diff_lq1[l] * diff_lk1[l])) - jnp.exp(jnp.sum(diff_lq2[l] * diff_lk2[l]))
               + lambda_init)
        y_c = _attn(proj, lam, diff_subln_g[l], bsz, seq, lambda_init, tq)
        x2 = _outproj(x2, y_a, y_b, y_c, w_out[l].astype(BF16), tm)
        x2 = _ffn(x2, norm_ff_g[l], w_ff1[l].astype(BF16), w_ff2[l].astype(BF16), norm_final_g,
                  l == depth - 1, tm, 512)
    return x2.reshape(bsz, seq, d)
```

```python
import functools
import math

import jax
import jax.numpy as jnp
import numpy as np
from jax import lax
from jax.experimental import pallas as pl
from jax.experimental.pallas import tpu as pltpu

F32 = jnp.float32
BF16 = jnp.bfloat16

D_MODEL = 2048
DEPTH = 2
CHUNK = 64
CHUNK_SHIFT = 6
A_WIDTH = 768
A_HEAD = 64
A_PAIRS = A_WIDTH // (2 * A_HEAD)
A_DECAY_LORA = 64
A_ICLR_LORA = 64
A_GATE_LORA = 128
A_IN = 3 * A_WIDTH + A_DECAY_LORA + A_ICLR_LORA + A_GATE_LORA
B_WIDTH = 512
B_GROUP = 16
B_GROUPS = B_WIDTH // B_GROUP
B_STATE = 64
C_WIDTH = 768
C_HEAD = 64
C_VHEAD = 2 * C_HEAD
C_HEADS = C_WIDTH // C_VHEAD
C_IN = 3 * C_WIDTH
N_IN = A_IN + B_WIDTH + C_IN
D_FF = 4 * D_MODEL
RMS_EPS = 1e-6
SUBLN_EPS = 1e-5
RWKV_GN_EPS = 64e-5
NEG_INF = -1e30
LOG2E = 1.4426950408889634

LANES = 128
VMEM_LIMIT = 56 * 1024 * 1024
S5_LC = 64
S5_ROW = S5_LC * B_GROUP


def _dot(a, b):
    return jnp.dot(a, b, preferred_element_type=F32)


def _dot_nt(a, b):
    return lax.dot_general(a, b, (((1,), (1,)), ((), ())), preferred_element_type=F32)


def _split3(x):
    hi = x.astype(BF16)
    r1 = x - hi.astype(F32)
    mid = r1.astype(BF16)
    lo = (r1 - mid.astype(F32)).astype(BF16)
    return hi, mid, lo


def _dot_x3(x, m):
    hi, mid, lo = _split3(x)
    return _dot(hi, m) + _dot(mid, m) + _dot(lo, m)


def _dot_x3_left(m, x):
    hi, mid, lo = _split3(x)
    return _dot(m, hi) + _dot(m, mid) + _dot(m, lo)


def _dot_hl(x, m_hi, m_lo):
    hi, mid, _ = _split3(x)
    return _dot(hi, m_hi) + _dot(mid, m_hi) + _dot(hi, m_lo)


def _rms(x, g, eps):
    return x * lax.rsqrt(jnp.mean(x * x, axis=-1, keepdims=True) + eps) * g


def _inproj_kernel(x_ref, g_ref, w_ref, o_ref, h_ref):
    @pl.when(pl.program_id(1) == 0)
    def _():
        h_ref[...] = _rms(x_ref[...], g_ref[...], RMS_EPS).astype(BF16)

    o_ref[...] = _dot(h_ref[...], w_ref[...]).astype(o_ref.dtype)


def _inproj(x2, g, w_bf, tm, tn, out_dtype):
    t, d = x2.shape
    n = w_bf.shape[1]
    return pl.pallas_call(
        _inproj_kernel,
        grid=(t // tm, n // tn),
        in_specs=[pl.BlockSpec((tm, d), lambda i, j: (i, 0)),
                  pl.BlockSpec((1, d), lambda i, j: (0, 0)),
                  pl.BlockSpec((d, tn), lambda i, j: (0, j))],
        out_specs=pl.BlockSpec((tm, tn), lambda i, j: (i, j)),
        out_shape=jax.ShapeDtypeStruct((t, n), out_dtype),
        scratch_shapes=[pltpu.VMEM((tm, d), BF16)],
        compiler_params=pltpu.CompilerParams(
            dimension_semantics=("arbitrary", "arbitrary"), vmem_limit_bytes=VMEM_LIMIT),
        name="inproj",
    )(x2, g.reshape(1, d), w_bf)


def _outproj_kernel(x_ref, ya_ref, yb_ref, yc_ref, wa_ref, wb_ref, wc_ref, o_ref):
    acc = _dot(ya_ref[...].astype(BF16), wa_ref[...])
    acc += _dot(yb_ref[...].astype(BF16), wb_ref[...])
    acc += _dot(yc_ref[...].astype(BF16), wc_ref[...])
    o_ref[...] = x_ref[...] + acc


def _outproj(x2, ya, yb, yc, w_bf, tm):
    t, d = x2.shape
    wa, wb, wc = w_bf[:A_WIDTH], w_bf[A_WIDTH:A_WIDTH + B_WIDTH], w_bf[A_WIDTH + B_WIDTH:]
    row = lambda w: pl.BlockSpec((tm, w), lambda i: (i, 0))
    full = lambda a: pl.BlockSpec(a.shape, lambda i: (0, 0))
    return pl.pallas_call(
        _outproj_kernel,
        grid=(t // tm,),
        in_specs=[row(d), row(A_WIDTH), row(B_WIDTH), row(C_WIDTH), full(wa), full(wb), full(wc)],
        out_specs=row(d),
        out_shape=jax.ShapeDtypeStruct((t, d), F32),
        compiler_params=pltpu.CompilerParams(
            dimension_semantics=("arbitrary",), vmem_limit_bytes=VMEM_LIMIT),
        name="outproj",
    )(x2, ya, yb, yc, wa, wb, wc)


def _ffn_kernel(x_ref, g_ref, w1_ref, w2_ref, gf_ref, o_ref, h_ref, acc_ref, *, final_norm):
    f = pl.program_id(1)

    @pl.when(f == 0)
    def _():
        h_ref[...] = _rms(x_ref[...], g_ref[...], RMS_EPS).astype(BF16)
        acc_ref[...] = jnp.zeros_like(acc_ref)

    a = _dot(h_ref[...], w1_ref[...])
    a = jnp.square(jnp.maximum(a, 0.0))
    acc_ref[...] += _dot(a.astype(BF16), w2_ref[...])

    @pl.when(f == pl.num_programs(1) - 1)
    def _():
        y = x_ref[...] + acc_ref[...]
        if final_norm:
            y = _rms(y, gf_ref[...], RMS_EPS)
        o_ref[...] = y


def _ffn(x2, g, w1_bf, w2_bf, gf, final_norm, tm, tf):
    t, d = x2.shape
    dff = w1_bf.shape[1]
    return pl.pallas_call(
        functools.partial(_ffn_kernel, final_norm=final_norm),
        grid=(t // tm, dff // tf),
        in_specs=[pl.BlockSpec((tm, d), lambda i, j: (i, 0)),
                  pl.BlockSpec((1, d), lambda i, j: (0, 0)),
                  pl.BlockSpec((d, tf), lambda i, j: (0, j)),
                  pl.BlockSpec((tf, d), lambda i, j: (j, 0)),
                  pl.BlockSpec((1, d), lambda i, j: (0, 0))],
        out_specs=pl.BlockSpec((tm, d), lambda i, j: (i, 0)),
        out_shape=jax.ShapeDtypeStruct((t, d), F32),
        scratch_shapes=[pltpu.VMEM((tm, d), BF16), pltpu.VMEM((tm, d), F32)],
        compiler_params=pltpu.CompilerParams(
            dimension_semantics=("arbitrary", "arbitrary"), vmem_limit_bytes=VMEM_LIMIT),
        name="ffn",
    )(x2, g.reshape(1, d), w1_bf, w2_bf, gf.reshape(1, d))


def _attn_kernel(q_ref, k_ref, vt_ref, lam_ref, g_ref, o_ref,
                 qst_ref, st0_ref, st1_ref, pt0_ref, pt1_ref, mx0_ref, mx1_ref, al0_ref, al1_ref,
                 m_ref, l_ref, acc_ref, *, tq, out_scale):
    qi = pl.program_id(2)
    st, pt, mx, al = (st0_ref, st1_ref), (pt0_ref, pt1_ref), (mx0_ref, mx1_ref), (al0_ref, al1_ref)
    qt = q_ref[...].astype(F32).T * (C_HEAD ** -0.5 * LOG2E)
    first = lax.broadcasted_iota(jnp.int32, qt.shape, 0) < C_HEAD
    qst_ref[:, 0:tq] = jnp.where(first, qt, 0.0).astype(BF16)
    qst_ref[:, tq:] = jnp.where(first, 0.0, qt).astype(BF16)
    m_ref[...] = jnp.full_like(m_ref, NEG_INF)
    l_ref[...] = jnp.zeros_like(l_ref)
    acc_ref[...] = jnp.zeros_like(acc_ref)
    pt1_ref[...] = jnp.zeros_like(pt1_ref)
    al1_ref[...] = jnp.ones_like(al1_ref)

    def scores(tile, slot, masked):
        k = k_ref[pl.ds(pl.multiple_of(tile * tq, tq), tq), :]
        s = _dot(k, qst_ref[...])
        if masked:
            kpos = lax.broadcasted_iota(jnp.int32, s.shape, 0)
            qpos = lax.broadcasted_iota(jnp.int32, s.shape, 1) & (tq - 1)
            s = jnp.where((kpos >> CHUNK_SHIFT) <= (qpos >> CHUNK_SHIFT), s, NEG_INF)
        st[slot][...] = s
        mx[slot][...] = jnp.max(s, axis=0, keepdims=True)

    def softmax(slot):
        m_prev = m_ref[...]
        m_new = jnp.maximum(m_prev, mx[slot][...])
        alpha = jnp.exp2(m_prev - m_new)
        m_ref[...] = m_new
        al[slot][...] = alpha
        for c in range(2 * tq // LANES):
            cols = slice(c * LANES, (c + 1) * LANES)
            m_col = m_new[:, cols]
            psum = jnp.zeros((1, LANES), F32)
            for r0 in range(0, tq, LANES):
                p = jnp.exp2(st[slot][r0:r0 + LANES, cols] - m_col)
                psum = psum + jnp.sum(p, axis=0, keepdims=True)
                pt[slot][r0:r0 + LANES, cols] = p.astype(BF16)
            l_ref[:, cols] = alpha[:, cols] * l_ref[:, cols] + psum

    def accumulate(tile, slot):
        acc_ref[...] = acc_ref[...] * al[slot][...] + _dot(vt_ref[tile], pt[slot][...])

    def step(n, slot):
        scores(n, 1 - slot, False)
        softmax(slot)
        accumulate(jnp.where(n <= 1, qi, n - 2), 1 - slot)

    scores(qi, 0, True)

    def body(i, carry):
        step(2 * i, 0)
        step(2 * i + 1, 1)
        return carry

    lax.fori_loop(0, qi >> 1, body, 0)
    tile_before_last = jnp.where(qi <= 1, qi, qi - 2)
    tile_last = jnp.where(qi == 0, qi, qi - 1)

    @pl.when((qi & 1) == 0)
    def _():
        softmax(0)
        accumulate(tile_before_last, 1)
        accumulate(tile_last, 0)

    @pl.when((qi & 1) == 1)
    def _():
        step(qi - 1, 0)
        softmax(1)
        accumulate(tile_before_last, 0)
        accumulate(tile_last, 1)

    o = acc_ref[...] * (1.0 / l_ref[...])
    o = o[:, 0:tq] - lam_ref[0:1, 0:1] * o[:, tq:]
    ms = jnp.mean(o * o, axis=0, keepdims=True)
    o = o * lax.rsqrt(ms + SUBLN_EPS) * (g_ref[...] * out_scale)
    o_ref[...] = o.T


def _attn(proj_c, lam, subln_g, bsz, seq, lambda_init, tq):
    t = proj_c.shape[0]
    nq = seq // tq
    koff = C_WIDTH // LANES
    v = proj_c[:, 2 * C_WIDTH:].reshape(bsz, nq, tq, C_HEADS, C_VHEAD)
    vt = v.transpose(0, 3, 1, 4, 2)
    return pl.pallas_call(
        functools.partial(_attn_kernel, tq=tq, out_scale=1.0 - lambda_init),
        grid=(bsz, C_HEADS, nq),
        in_specs=[pl.BlockSpec((tq, LANES), lambda b, h, i: (b * nq + i, h)),
                  pl.BlockSpec((seq, LANES), lambda b, h, i: (b, koff + h)),
                  pl.BlockSpec((None, None, nq, C_VHEAD, tq), lambda b, h, i: (b, h, 0, 0, 0)),
                  pl.BlockSpec((1, LANES), lambda b, h, i: (0, 0)),
                  pl.BlockSpec((None, C_VHEAD, 1), lambda b, h, i: (h, 0, 0))],
        out_specs=pl.BlockSpec((tq, LANES), lambda b, h, i: (b * nq + i, h)),
        out_shape=jax.ShapeDtypeStruct((t, C_WIDTH), F32),
        scratch_shapes=[pltpu.VMEM((C_VHEAD, 2 * tq), BF16)]
        + [pltpu.VMEM((tq, 2 * tq), F32)] * 2 + [pltpu.VMEM((tq, 2 * tq), BF16)] * 2
        + [pltpu.VMEM((1, 2 * tq), F32)] * 6 + [pltpu.VMEM((C_VHEAD, 2 * tq), F32)],
        compiler_params=pltpu.CompilerParams(
            dimension_semantics=("arbitrary", "arbitrary", "arbitrary"), vmem_limit_bytes=VMEM_LIMIT),
        name="diffattn",
    )(proj_c, proj_c, vt, jnp.broadcast_to(lam.reshape(1, 1), (1, LANES)).astype(F32),
      subln_g.reshape(C_HEADS, C_VHEAD, 1))


def _s5_params(a_re, a_im, log_dt, b_re, b_im, c_re, c_im, d_skip, w_glu, b_glu, beta, n_levels):
    hp = lax.Precision.HIGHEST
    g, p = a_re.shape
    dt = jnp.exp(log_dt)[:, None]
    mag = jnp.exp(dt * a_re)
    abr, abi = mag * jnp.cos(dt * a_im), mag * jnp.sin(dt * a_im)
    den = a_re * a_re + a_im * a_im
    zr = ((abr - 1.0) * a_re + abi * a_im) / den
    zi = (abi * a_re - (abr - 1.0) * a_im) / den
    bbr = zr[..., None] * b_re - zi[..., None] * b_im
    bbi = zr[..., None] * b_im + zi[..., None] * b_re
    pwr, pwi = jnp.ones((g, 1, p), F32), jnp.zeros((g, 1, p), F32)
    qr, qi = abr[:, None, :], abi[:, None, :]
    n = 1
    while n < S5_LC:
        pwr, pwi = (jnp.concatenate([pwr, pwr * qr - pwi * qi], 1),
                    jnp.concatenate([pwi, pwr * qi + pwi * qr], 1))
        qr, qi = qr * qr - qi * qi, 2.0 * qr * qi
        n *= 2
    lev1, lev2 = [], []
    for _ in range(n_levels):
        lev1.append(jnp.concatenate([qr, qr], -1))
        lev2.append(jnp.concatenate([-qi, qi], -1))
        qr, qi = qr * qr - qi * qi, 2.0 * qr * qi
    lev1 = jnp.concatenate(lev1, 1)
    lev2 = jnp.concatenate(lev2, 1)
    pw1r = jnp.concatenate([pwr[:, 1:], lev1[:, :1, :p]], 1)
    pw1i = jnp.concatenate([pwi[:, 1:], lev2[:, :1, p:]], 1)
    cpr = c_re[:, None] * pwr[:, :, None, :] - c_im[:, None] * pwi[:, :, None, :]
    cpi = c_re[:, None] * pwi[:, :, None, :] + c_im[:, None] * pwr[:, :, None, :]
    kj = (jnp.einsum('gjhp,gpk->gjhk', cpr, bbr, precision=hp)
          - jnp.einsum('gjhp,gpk->gjhk', cpi, bbi, precision=hp))
    kj = kj.at[:, 0].add(d_skip[:, :, None] * jnp.eye(B_GROUP, dtype=F32))
    lag = jnp.arange(S5_LC)[None, :] - jnp.arange(S5_LC)[:, None]
    kt = kj[:, jnp.clip(lag, 0, S5_LC - 1)] * (lag >= 0)[None, :, :, None, None]
    kmat = kt.transpose(0, 1, 4, 2, 3).reshape(g, S5_ROW, S5_ROW)
    rr, ri = pwr[:, ::-1], pwi[:, ::-1]
    bpr = rr[:, :, :, None] * bbr[:, None] - ri[:, :, :, None] * bbi[:, None]
    bpi = rr[:, :, :, None] * bbi[:, None] + ri[:, :, :, None] * bbr[:, None]
    bmat = jnp.concatenate([bpr.transpose(0, 1, 3, 2), bpi.transpose(0, 1, 3, 2)], -1)
    bmat = bmat.reshape(g, S5_ROW, 2 * p)
    cqr = c_re[:, None] * pw1r[:, :, None, :] - c_im[:, None] * pw1i[:, :, None, :]
    cqi = c_re[:, None] * pw1i[:, :, None, :] + c_im[:, None] * pw1r[:, :, None, :]
    cmat = jnp.concatenate([cqr.transpose(0, 3, 1, 2), -cqi.transpose(0, 3, 1, 2)], 1)
    cmat = cmat.reshape(g, 2 * p, S5_ROW)
    rep = LANES // B_GROUP
    eye = jnp.eye(rep, dtype=F32)
    glu = jnp.einsum('ab,ghk->gahbk', eye, w_glu).reshape(g, LANES, LANES)
    bglu = jnp.tile(b_glu, (1, rep)).reshape(g, 1, LANES)
    betat = jnp.tile(beta.reshape(g, B_GROUP), (1, S5_LC)).reshape(g, 1, S5_ROW)
    return (kmat.astype(BF16), bmat.astype(BF16), cmat.astype(BF16), glu.astype(BF16),
            bglu, betat, lev1, lev2)


def _gelu_tanh(x):
    return 0.5 * x * (1.0 + jnp.tanh(math.sqrt(2.0 / math.pi) * (x + 0.044715 * (x * x * x))))


def _s5_kernel(u_ref, kmat_ref, bmat_ref, cmat_ref, glu_ref, bglu_ref, beta_ref, lev1_ref, lev2_ref,
               mean_ref, o_ref, *, rows_per_batch, n_levels):
    u = u_ref[...].astype(BF16)
    y = _dot(u, kmat_ref[...])
    x = _dot(u, bmat_ref[...])
    rpos = lax.broadcasted_iota(jnp.int32, x.shape, 0) & (rows_per_batch - 1)
    for lv in range(n_levels):
        d = 1 << lv
        if d >= rows_per_batch:
            break
        sh = jnp.where(rpos >= d, pltpu.roll(x, d, axis=0), 0.0)
        sw = pltpu.roll(sh, B_STATE, axis=1)
        x = x + lev1_ref[lv:lv + 1, :] * sh + lev2_ref[lv:lv + 1, :] * sw
    xs = jnp.where(rpos >= 1, pltpu.roll(x, 1, axis=0), 0.0)
    y = y + _dot(xs.astype(BF16), cmat_ref[...])
    y = _gelu_tanh(y)
    glu = glu_ref[...]
    mean = mean_ref[...]
    for c in range(S5_ROW // LANES):
        sl = slice(c * LANES, (c + 1) * LANES)
        yc = y[:, sl]
        z = _dot(yc.astype(BF16), glu) + bglu_ref[...]
        yc = yc * jax.nn.sigmoid(z)
        ms = _dot_x3(yc * yc, mean)
        o_ref[:, sl] = yc * lax.rsqrt(ms + RMS_EPS) * beta_ref[:, sl]


def _s5(proj, params, bsz, seq):
    t = proj.shape[0]
    rows = t // S5_LC
    rows_per_batch = seq // S5_LC
    n_levels = max(1, (rows_per_batch - 1).bit_length())
    kmat, bmat, cmat, glu, bglu, betat, lev1, lev2 = params
    u = proj[:, A_IN:A_IN + B_WIDTH].reshape(rows, S5_LC, B_GROUPS, B_GROUP)
    u = u.transpose(2, 0, 1, 3).reshape(B_GROUPS, rows, S5_ROW)
    rep = LANES // B_GROUP
    mean = jnp.kron(jnp.eye(rep, dtype=F32), jnp.full((B_GROUP, B_GROUP), 1.0 / B_GROUP, F32)).astype(BF16)
    per_g = lambda a: pl.BlockSpec((None,) + a.shape[1:], lambda g: (g,) + (0,) * (a.ndim - 1))
    y = pl.pallas_call(
        functools.partial(_s5_kernel, rows_per_batch=rows_per_batch, n_levels=n_levels),
        grid=(B_GROUPS,),
        in_specs=[per_g(u), per_g(kmat), per_g(bmat), per_g(cmat), per_g(glu), per_g(bglu), per_g(betat),
                  per_g(lev1), per_g(lev2), pl.BlockSpec(mean.shape, lambda g: (0, 0))],
        out_specs=pl.BlockSpec((None, rows, S5_ROW), lambda g: (g, 0, 0)),
        out_shape=jax.ShapeDtypeStruct((B_GROUPS, rows, S5_ROW), F32),
        compiler_params=pltpu.CompilerParams(
            dimension_semantics=("arbitrary",), vmem_limit_bytes=VMEM_LIMIT),
        name="s5",
    )(u, kmat, bmat, cmat, glu, bglu, betat, lev1, lev2, mean)
    return y.reshape(B_GROUPS, rows, S5_LC, B_GROUP).transpose(1, 2, 0, 3).reshape(t, B_WIDTH)


def _rwkv_chunks(pairs, strict, incl, first):
    c = CHUNK
    zero = jnp.zeros((), BF16)
    st = lambda x: jnp.concatenate([jnp.where(first, x, zero), jnp.where(first, zero, x)], axis=0)
    cat = lambda xs: jnp.concatenate(xs, axis=0)
    v_st = [st(p[4]) for p in pairs]
    gmat = [_dot_nt(cat([p[0], p[0], p[1], p[1]]), cat([st(p[2]), st(p[3])])) for p in pairs]
    ls = [_dot_nt(cat([st(p[0]), st(p[1])]), p[8].astype(BF16)) for p in pairs]
    a_ab = [jnp.where(strict, g[:2 * c, :2 * c], 0.0).astype(BF16) for g in gmat]
    a_k = [cat([jnp.where(strict, g[:2 * c, 2 * c:], 0.0), jnp.where(incl, g[2 * c:, 2 * c:], 0.0)]).astype(BF16)
           for g in gmat]
    a_rb = [jnp.where(incl, g[2 * c:, :2 * c], 0.0).astype(BF16) for g in gmat]
    av = [_dot(ak, vs) for ak, vs in zip(a_k, v_st)]
    u = [l[:2 * c] + a[:2 * c] for l, a in zip(ls, av)]
    pw = a_ab
    n = 1
    while n < c:
        if 2 * n >= c:
            x = [_dot(p_, u_.astype(BF16)) for p_, u_ in zip(pw, u)]
            u = [u_ + x_ for u_, x_ in zip(u, x)]
        else:
            x = [_dot(p_, jnp.concatenate([u_.astype(BF16), p_], axis=1)) for p_, u_ in zip(pw, u)]
            u = [u_ + x_[:, :2 * c] for u_, x_ in zip(u, x)]
            pw = [x_[:, 2 * c:].astype(BF16) for x_ in x]
        n *= 2
    yst = [l[2 * c:] + a[2 * c:] + _dot(arb, u_.astype(BF16)) for l, a, arb, u_ in zip(ls, av, a_rb, u)]
    ys = [y_[:c] + y_[c:] for y_ in yst]
    s_new = []
    for p, u_, vs in zip(pairs, u, v_st):
        lhs_t = cat([u_, vs.astype(F32)]).T.astype(BF16)
        s_new.append(p[8] * p[7] + _dot(lhs_t, cat([st(p[5]), st(p[6])])))
    return ys, s_new


def _rwkv_kernel(p_ref, mu_ref, w0_ref, a0_ref, wch_ref, wcl_ref, guh_ref, gul_ref, kkp_ref, kap_ref,
                 rk_ref, gng_ref, gnb_ref, seg_ref, tri_ref, o_ref,
                 state_ref, carry_ref, al_s, rt_s, be_s, kh_s, v_s, bc_s, kc_s, gend_s, y_s, bonus_s,
                 gate_s, *, tb):
    ti = pl.program_id(1)

    @pl.when(ti == 0)
    def _():
        state_ref[...] = jnp.zeros_like(state_ref)
        carry_ref[...] = jnp.zeros_like(carry_ref)

    p = p_ref[...]
    row = lax.broadcasted_iota(jnp.int32, p.shape, 0)
    prev = jnp.where(row == 0, carry_ref[0:1, :], pltpu.roll(p, 1, axis=0))
    carry_ref[0:1, :] = p[tb - 1:tb, :]
    xs = p + (prev - p) * mu_ref[...]
    r = xs[:, 0:A_WIDTH]
    k = xs[:, A_WIDTH:2 * A_WIDTH]
    v = xs[:, 2 * A_WIDTH:3 * A_WIDTH]
    z = xs[:, 3 * A_WIDTH:3 * A_WIDTH + LANES]
    lane = lax.broadcasted_iota(jnp.int32, z.shape, 1)
    z = jnp.where(lane < A_DECAY_LORA, jnp.tanh(z), z)
    wa = _dot_hl(z, wch_ref[...], wcl_ref[...])
    w = -jax.nn.softplus(-(w0_ref[...] + wa[:, :A_WIDTH])) - 0.5
    lw = -jnp.exp(w)
    a = jax.nn.sigmoid(a0_ref[...] + wa[:, A_WIDTH:])
    gate = _dot_hl(jax.nn.sigmoid(xs[:, 3 * A_WIDTH + LANES:]), guh_ref[...], gul_ref[...])
    kk = k * kkp_ref[...]
    k = k * (1.0 + (a - 1.0) * kap_ref[...])
    seg = seg_ref[...]
    blocks = [slice(j * LANES, (j + 1) * LANES) for j in range(A_PAIRS)]
    segsum = lambda x: jnp.concatenate([_dot_x3(x[:, b], seg) for b in blocks], axis=1)
    kk = kk * lax.rsqrt(segsum(kk * kk) + 1e-12)
    gate_s[...] = gate
    bonus_s[...] = segsum(r * k * rk_ref[...]) * v * gate
    v_s[...] = v.astype(BF16)
    tri = tri_ref[...]
    for c in range(tb // CHUNK):
        rows = slice(c * CHUNK, (c + 1) * CHUNK)
        lwc = lw[rows, :]
        cum = _dot_x3_left(tri, lwc)
        g = jnp.exp(cum)
        gi = jnp.exp(-cum)
        g_end = g[CHUNK - 1:CHUNK, :]
        be = kk[rows, :] * a[rows, :] * gi
        kh = k[rows, :] * gi
        al_s[rows, :] = (-kk[rows, :] * jnp.exp(cum - lwc)).astype(BF16)
        rt_s[rows, :] = (r[rows, :] * g).astype(BF16)
        be_s[rows, :] = be.astype(BF16)
        kh_s[rows, :] = kh.astype(BF16)
        bc_s[rows, :] = (be * g_end).astype(BF16)
        kc_s[rows, :] = (kh * g_end).astype(BF16)
        gend_s[8 * c:8 * c + 1, :] = g_end

    i2 = lax.broadcasted_iota(jnp.int32, (2 * CHUNK, 2 * CHUNK), 0)
    j2 = lax.broadcasted_iota(jnp.int32, (2 * CHUNK, 2 * CHUNK), 1)
    same = (i2 >> CHUNK_SHIFT) == (j2 >> CHUNK_SHIFT)
    strict = same & ((j2 & (CHUNK - 1)) < (i2 & (CHUNK - 1)))
    incl = same & ((j2 & (CHUNK - 1)) <= (i2 & (CHUNK - 1)))
    first = lax.broadcasted_iota(jnp.int32, (CHUNK, LANES), 1) < A_HEAD

    def chunk_body(c, carry):
        rows = pl.ds(pl.multiple_of(c * CHUNK, CHUNK), CHUNK)
        grow = pl.ds(pl.multiple_of(c * 8, 8), 1)
        pairs = [(al_s[rows, b], rt_s[rows, b], be_s[rows, b], kh_s[rows, b], v_s[rows, b], bc_s[rows, b],
                  kc_s[rows, b], gend_s[grow, b], state_ref[j]) for j, b in enumerate(blocks)]
        ys, s_new = _rwkv_chunks(pairs, strict, incl, first)
        for j, b in enumerate(blocks):
            y_s[rows, b] = ys[j]
            state_ref[j] = s_new[j]
        return carry

    lax.fori_loop(0, tb // CHUNK, chunk_body, 0)

    y = y_s[...]
    inv = 1.0 / A_HEAD
    mean = segsum(y) * inv
    yc = y - mean
    var = segsum(yc * yc) * inv
    y = yc * lax.rsqrt(var + RWKV_GN_EPS) * gng_ref[...] + gnb_ref[...]
    o_ref[...] = y * gate_s[...] + bonus_s[...]


def _rwkv(proj, prm, bsz, seq, tb):
    t = proj.shape[0]
    nt = seq // tb
    (mu, w0, a0, wch, wcl, guh, gul, kkp, kap, rk, gng, gnb) = prm
    seg = jnp.kron(jnp.eye(2, dtype=F32), jnp.ones((A_HEAD, A_HEAD), F32)).astype(BF16)
    tri = jnp.tril(jnp.ones((CHUNK, CHUNK), F32)).astype(BF16)
    full = lambda a_: pl.BlockSpec(a_.shape, lambda b, i: (0,) * a_.ndim)
    vm = lambda dt: pltpu.VMEM((tb, A_WIDTH), dt)
    return pl.pallas_call(
        functools.partial(_rwkv_kernel, tb=tb),
        grid=(bsz, nt),
        in_specs=[pl.BlockSpec((tb, A_IN), lambda b, i: (b * nt + i, 0))]
        + [full(x) for x in (mu, w0, a0, wch, wcl, guh, gul, kkp, kap, rk, gng, gnb, seg, tri)],
        out_specs=pl.BlockSpec((tb, A_WIDTH), lambda b, i: (b * nt + i, 0)),
        out_shape=jax.ShapeDtypeStruct((t, A_WIDTH), F32),
        scratch_shapes=[pltpu.VMEM((A_PAIRS, LANES, LANES), F32), pltpu.VMEM((8, A_IN), F32)]
        + [vm(BF16) for _ in range(7)] + [pltpu.VMEM((tb // CHUNK * 8, A_WIDTH), F32)]
        + [vm(F32) for _ in range(3)],
        compiler_params=pltpu.CompilerParams(
            dimension_semantics=("arbitrary", "arbitrary"), vmem_limit_bytes=VMEM_LIMIT),
        name="rwkv7",
    )(proj, mu, w0, a0, wch, wcl, guh, gul, kkp, kap, rk, gng, gnb, seg, tri)


def _rwkv_params(mu, w0, w_up, a0, a_up, g_up, k_k, k_a, r_k, gn_g, gn_b):
    row = lambda x: x.reshape(1, -1).astype(F32)
    zeros = jnp.zeros_like(w_up)
    wc = jnp.concatenate([jnp.concatenate([w_up, zeros], 1), jnp.concatenate([zeros, a_up], 1)], 0)
    hl = lambda m: (m.astype(BF16), (m - m.astype(BF16).astype(F32)).astype(BF16))
    wch, wcl = hl(wc)
    guh, gul = hl(g_up)
    return (row(mu), row(w0), row(a0), wch, wcl, guh, gul, row(k_k), row(k_a), row(r_k), row(gn_g),
            row(gn_b))


def _pick(n, pref):
    t = min(n, pref)
    while n % t:
        t //= 2
    return t


def kernel(x, norm_mix_g, w_in, rwkv_mu, rwkv_w0, rwkv_w_up, rwkv_a0, rwkv_a_up, rwkv_g_up, rwkv_k_k, rwkv_k_a, rwkv_r_k, rwkv_gn_g, rwkv_gn_b, s5_a_re, s5_a_im, s5_log_dt, s5_b_re, s5_b_im, s5_c_re, s5_c_im, s5_d, s5_w_glu, s5_b_glu, s5_beta, diff_lq1, diff_lk1, diff_lq2, diff_lk2, diff_subln_g, w_out, norm_ff_g, w_ff1, w_ff2, norm_final_g):
    bsz, seq, d = x.shape
    t = bsz * seq
    depth = w_in.shape[0]
    x2 = x.reshape(t, d).astype(F32)
    tm = _pick(t, 512)
    tm_in = _pick(t, 1024)
    tq = _pick(seq, 512)
    tb = _pick(seq, 256)
    assert tq & (tq - 1) == 0 and (seq // S5_LC) & (seq // S5_LC - 1) == 0 and tb % CHUNK == 0
    rows_per_batch = seq // S5_LC
    n_levels = max(1, (rows_per_batch - 1).bit_length())
    for l in range(depth):
        w_in_bf = w_in[l].astype(BF16)
        proj = _inproj(x2, norm_mix_g[l], w_in_bf[:, :A_IN + B_WIDTH], tm_in, 768, F32)
        proj_c = _inproj(x2, norm_mix_g[l], w_in_bf[:, A_IN + B_WIDTH:], tm_in, 768, BF16)
        prm = _rwkv_params(rwkv_mu[l], rwkv_w0[l], rwkv_w_up[l], rwkv_a0[l], rwkv_a_up[l], rwkv_g_up[l],
                           rwkv_k_k[l], rwkv_k_a[l], rwkv_r_k[l].reshape(-1), rwkv_gn_g[l], rwkv_gn_b[l])
        y_a = _rwkv(proj, prm, bsz, seq, tb)
        s5p = _s5_params(s5_a_re[l], s5_a_im[l], s5_log_dt[l], s5_b_re[l], s5_b_im[l], s5_c_re[l],
                         s5_c_im[l], s5_d[l], s5_w_glu[l], s5_b_glu[l], s5_beta[l], n_levels)
        y_b = _s5(proj, s5p, bsz, seq)
        lambda_init = 0.8 - 0.6 * math.exp(-0.3 * l)
        lam = (jnp.exp(jnp.sum(diff_lq1[l] * diff_lk1[l])) - jnp.exp(jnp.sum(diff_lq2[l] * diff_lk2[l]))
               + lambda_init)
        y_c = _attn(proj_c, lam, diff_subln_g[l], bsz, seq, lambda_init, tq)
        x2 = _outproj(x2, y_a, y_b, y_c, w_out[l].astype(BF16), tm)
        x2 = _ffn(x2, norm_ff_g[l], w_ff1[l].astype(BF16), w_ff2[l].astype(BF16), norm_final_g,
                  l == depth - 1, tm, 1024)
    return x2.reshape(bsz, seq, d)
```

```python
import functools
import math

import jax
import jax.numpy as jnp
import numpy as np
from jax import lax
from jax.experimental import pallas as pl
from jax.experimental.pallas import tpu as pltpu

F32 = jnp.float32
BF16 = jnp.bfloat16

D_MODEL = 2048
DEPTH = 2
CHUNK = 64
CHUNK_SHIFT = 6
A_WIDTH = 768
A_HEAD = 64
A_PAIRS = A_WIDTH // (2 * A_HEAD)
A_DECAY_LORA = 64
A_ICLR_LORA = 64
A_GATE_LORA = 128
A_IN = 3 * A_WIDTH + A_DECAY_LORA + A_ICLR_LORA + A_GATE_LORA
B_WIDTH = 512
B_GROUP = 16
B_GROUPS = B_WIDTH // B_GROUP
B_STATE = 64
C_WIDTH = 768
C_HEAD = 64
C_VHEAD = 2 * C_HEAD
C_HEADS = C_WIDTH // C_VHEAD
C_IN = 3 * C_WIDTH
N_IN = A_IN + B_WIDTH + C_IN
D_FF = 4 * D_MODEL
RMS_EPS = 1e-6
SUBLN_EPS = 1e-5
RWKV_GN_EPS = 64e-5
NEG_INF = -1e30
LOG2E = 1.4426950408889634

LANES = 128
VMEM_LIMIT = 56 * 1024 * 1024
ATTN_CG = 256
ATTN_VROWS = C_VHEAD + 16
S5_LC = 64
S5_ROW = S5_LC * B_GROUP


def _dot(a, b):
    return jnp.dot(a, b, preferred_element_type=F32)


def _dot_nt(a, b):
    return lax.dot_general(a, b, (((1,), (1,)), ((), ())), preferred_element_type=F32)


def _split3(x):
    hi = x.astype(BF16)
    r1 = x - hi.astype(F32)
    mid = r1.astype(BF16)
    lo = (r1 - mid.astype(F32)).astype(BF16)
    return hi, mid, lo


def _dot_x3(x, m):
    hi, mid, lo = _split3(x)
    return _dot(hi, m) + _dot(mid, m) + _dot(lo, m)


def _dot_x2(x, m):
    hi = x.astype(BF16)
    mid = (x - hi.astype(F32)).astype(BF16)
    return _dot(hi, m) + _dot(mid, m)


def _dot_x3_left(m, x):
    hi, mid, lo = _split3(x)
    return _dot(m, hi) + _dot(m, mid) + _dot(m, lo)


def _dot_hl(x, m_hi, m_lo):
    hi, mid, _ = _split3(x)
    return _dot(hi, m_hi) + _dot(mid, m_hi) + _dot(hi, m_lo)


def _rms(x, g, eps):
    return x * lax.rsqrt(jnp.mean(x * x, axis=-1, keepdims=True) + eps) * g


def _inproj_kernel(x_ref, g_ref, w_ref, o32_ref, o16_ref, h_ref, *, n32):
    j = pl.program_id(1)

    @pl.when(j == 0)
    def _():
        h_ref[...] = _rms(x_ref[...], g_ref[...], RMS_EPS).astype(BF16)

    @pl.when(j < n32)
    def _():
        o32_ref[...] = _dot(h_ref[...], w_ref[...])

    @pl.when(j >= n32)
    def _():
        o16_ref[...] = _dot(h_ref[...], w_ref[...]).astype(BF16)


def _inproj(x2, g, w_bf, tm, tn, n_f32):
    t, d = x2.shape
    n = w_bf.shape[1]
    n32 = n_f32 // tn
    return pl.pallas_call(
        functools.partial(_inproj_kernel, n32=n32),
        grid=(t // tm, n // tn),
        in_specs=[pl.BlockSpec((tm, d), lambda i, j: (i, 0)),
                  pl.BlockSpec((1, d), lambda i, j: (0, 0)),
                  pl.BlockSpec((d, tn), lambda i, j: (0, j))],
        out_specs=[pl.BlockSpec((tm, tn), lambda i, j: (i, jnp.minimum(j, n32 - 1))),
                   pl.BlockSpec((tm, tn), lambda i, j: (i, jnp.maximum(j - n32, 0)))],
        out_shape=[jax.ShapeDtypeStruct((t, n_f32), F32), jax.ShapeDtypeStruct((t, n - n_f32), BF16)],
        scratch_shapes=[pltpu.VMEM((tm, d), BF16)],
        compiler_params=pltpu.CompilerParams(
            dimension_semantics=("arbitrary", "arbitrary"), vmem_limit_bytes=VMEM_LIMIT),
        name="inproj",
    )(x2, g.reshape(1, d), w_bf)


def _outproj_kernel(x_ref, ya_ref, yb_ref, yc_ref, wa_ref, wb_ref, wc_ref, o_ref):
    acc = _dot(ya_ref[...].astype(BF16), wa_ref[...])
    acc += _dot(yb_ref[...].astype(BF16), wb_ref[...])
    acc += _dot(yc_ref[...].astype(BF16), wc_ref[...])
    o_ref[...] = x_ref[...] + acc


def _outproj(x2, ya, yb, yc, w_bf, tm):
    t, d = x2.shape
    wa, wb, wc = w_bf[:A_WIDTH], w_bf[A_WIDTH:A_WIDTH + B_WIDTH], w_bf[A_WIDTH + B_WIDTH:]
    row = lambda w: pl.BlockSpec((tm, w), lambda i: (i, 0))
    full = lambda a: pl.BlockSpec(a.shape, lambda i: (0, 0))
    return pl.pallas_call(
        _outproj_kernel,
        grid=(t // tm,),
        in_specs=[row(d), row(A_WIDTH), row(B_WIDTH), row(C_WIDTH), full(wa), full(wb), full(wc)],
        out_specs=row(d),
        out_shape=jax.ShapeDtypeStruct((t, d), F32),
        compiler_params=pltpu.CompilerParams(
            dimension_semantics=("arbitrary",), vmem_limit_bytes=VMEM_LIMIT),
        name="outproj",
    )(x2, ya, yb, yc, wa, wb, wc)


def _ffn_kernel(x_ref, g_ref, w1_ref, w2_ref, gf_ref, o_ref, h_ref, acc_ref, *, final_norm):
    f = pl.program_id(1)

    @pl.when(f == 0)
    def _():
        h_ref[...] = _rms(x_ref[...], g_ref[...], RMS_EPS).astype(BF16)
        acc_ref[...] = jnp.zeros_like(acc_ref)

    a = _dot(h_ref[...], w1_ref[...])
    a = jnp.square(jnp.maximum(a, 0.0))
    acc_ref[...] += _dot(a.astype(BF16), w2_ref[...])

    @pl.when(f == pl.num_programs(1) - 1)
    def _():
        y = x_ref[...] + acc_ref[...]
        if final_norm:
            y = _rms(y, gf_ref[...], RMS_EPS)
        o_ref[...] = y


def _ffn(x2, g, w1_bf, w2_bf, gf, final_norm, tm, tf):
    t, d = x2.shape
    dff = w1_bf.shape[1]
    return pl.pallas_call(
        functools.partial(_ffn_kernel, final_norm=final_norm),
        grid=(t // tm, dff // tf),
        in_specs=[pl.BlockSpec((tm, d), lambda i, j: (i, 0)),
                  pl.BlockSpec((1, d), lambda i, j: (0, 0)),
                  pl.BlockSpec((d, tf), lambda i, j: (0, j)),
                  pl.BlockSpec((tf, d), lambda i, j: (j, 0)),
                  pl.BlockSpec((1, d), lambda i, j: (0, 0))],
        out_specs=pl.BlockSpec((tm, d), lambda i, j: (i, 0)),
        out_shape=jax.ShapeDtypeStruct((t, d), F32),
        scratch_shapes=[pltpu.VMEM((tm, d), BF16), pltpu.VMEM((tm, d), F32)],
        compiler_params=pltpu.CompilerParams(
            dimension_semantics=("arbitrary", "arbitrary"), vmem_limit_bytes=VMEM_LIMIT),
        name="ffn",
    )(x2, g.reshape(1, d), w1_bf, w2_bf, gf.reshape(1, d))


def _attn_kernel(q_ref, k_ref, vt_ref, lam_ref, g_ref, o_ref,
                 qst_ref, st0_ref, st1_ref, pt0_ref, pt1_ref, mx0_ref, mx1_ref, al0_ref, al1_ref,
                 m_ref, acc_ref, *, tq, out_scale):
    qi = pl.program_id(2)
    st, pt, mx, al = (st0_ref, st1_ref), (pt0_ref, pt1_ref), (mx0_ref, mx1_ref), (al0_ref, al1_ref)
    qt = q_ref[...].astype(F32).T * (C_HEAD ** -0.5 * LOG2E)
    first = lax.broadcasted_iota(jnp.int32, qt.shape, 0) < C_HEAD
    qst_ref[:, 0:tq] = jnp.where(first, qt, 0.0).astype(BF16)
    qst_ref[:, tq:] = jnp.where(first, 0.0, qt).astype(BF16)
    m_ref[...] = jnp.full_like(m_ref, NEG_INF)
    acc_ref[...] = jnp.zeros_like(acc_ref)
    pt1_ref[...] = jnp.zeros_like(pt1_ref)
    al1_ref[...] = jnp.ones_like(al1_ref)
    groups = [slice(g * ATTN_CG, (g + 1) * ATTN_CG) for g in range(2 * tq // ATTN_CG)]

    def scores(tile, slot, masked, cg):
        k = k_ref[pl.ds(pl.multiple_of(tile * tq, tq), tq), :]
        s = _dot(k, qst_ref[:, cg])
        if masked:
            kpos = lax.broadcasted_iota(jnp.int32, s.shape, 0)
            qpos = (lax.broadcasted_iota(jnp.int32, s.shape, 1) + cg.start) & (tq - 1)
            s = jnp.where((kpos >> CHUNK_SHIFT) <= (qpos >> CHUNK_SHIFT), s, NEG_INF)
        st[slot][:, cg] = s
        mx[slot][:, cg] = jnp.max(s, axis=0, keepdims=True)

    def softmax(slot, cg):
        m_prev = m_ref[:, cg]
        m_new = jnp.maximum(m_prev, mx[slot][:, cg])
        m_ref[:, cg] = m_new
        al[slot][:, cg] = jnp.exp2(m_prev - m_new)
        for c0 in range(cg.start, cg.stop, LANES):
            cols = slice(c0, c0 + LANES)
            m_col = m_new[:, c0 - cg.start:c0 - cg.start + LANES]
            for r0 in range(0, tq, LANES):
                p = jnp.exp2(st[slot][r0:r0 + LANES, cols] - m_col)
                pt[slot][r0:r0 + LANES, cols] = p.astype(BF16)

    def accumulate(tile, slot, cg):
        acc_ref[:, cg] = acc_ref[:, cg] * al[slot][:, cg] + _dot(vt_ref[tile], pt[slot][:, cg])

    def step(n, slot):
        prev_tile = jnp.where(n <= 1, qi, n - 2)
        for cg in groups:
            scores(n, 1 - slot, False, cg)
            softmax(slot, cg)
            accumulate(prev_tile, 1 - slot, cg)

    for cg in groups:
        scores(qi, 0, True, cg)

    def body(i, carry):
        step(2 * i, 0)
        step(2 * i + 1, 1)
        return carry

    lax.fori_loop(0, qi >> 1, body, 0)
    tile_before_last = jnp.where(qi <= 1, qi, qi - 2)
    tile_last = jnp.where(qi == 0, qi, qi - 1)

    @pl.when((qi & 1) == 0)
    def _():
        for cg in groups:
            softmax(0, cg)
            accumulate(tile_before_last, 1, cg)
            accumulate(tile_last, 0, cg)

    @pl.when((qi & 1) == 1)
    def _():
        step(qi - 1, 0)
        for cg in groups:
            softmax(1, cg)
            accumulate(tile_before_last, 0, cg)
            accumulate(tile_last, 1, cg)

    o = acc_ref[0:C_VHEAD, :] * (1.0 / acc_ref[C_VHEAD:C_VHEAD + 1, :])
    o = o[:, 0:tq] - lam_ref[0:1, 0:1] * o[:, tq:]
    ms = jnp.mean(o * o, axis=0, keepdims=True)
    o = o * lax.rsqrt(ms + SUBLN_EPS) * (g_ref[...] * out_scale)
    o_ref[...] = o.T


def _attn(proj_c, lam, subln_g, bsz, seq, lambda_init, tq):
    t = proj_c.shape[0]
    nq = seq // tq
    koff = C_WIDTH // LANES
    v = proj_c[:, 2 * C_WIDTH:].reshape(bsz, nq, tq, C_HEADS, C_VHEAD)
    vt = v.transpose(0, 3, 1, 4, 2)
    ones_rows = jnp.zeros((bsz, C_HEADS, nq, ATTN_VROWS - C_VHEAD, tq), BF16).at[:, :, :, 0, :].set(1.0)
    vt = jnp.concatenate([vt, ones_rows], axis=3)
    return pl.pallas_call(
        functools.partial(_attn_kernel, tq=tq, out_scale=1.0 - lambda_init),
        grid=(bsz, C_HEADS, nq),
        in_specs=[pl.BlockSpec((tq, LANES), lambda b, h, i: (b * nq + i, h)),
                  pl.BlockSpec((seq, LANES), lambda b, h, i: (b, koff + h)),
                  pl.BlockSpec((None, None, nq, ATTN_VROWS, tq), lambda b, h, i: (b, h, 0, 0, 0)),
                  pl.BlockSpec((1, LANES), lambda b, h, i: (0, 0)),
                  pl.BlockSpec((None, C_VHEAD, 1), lambda b, h, i: (h, 0, 0))],
        out_specs=pl.BlockSpec((tq, LANES), lambda b, h, i: (b * nq + i, h)),
        out_shape=jax.ShapeDtypeStruct((t, C_WIDTH), F32),
        scratch_shapes=[pltpu.VMEM((C_VHEAD, 2 * tq), BF16)]
        + [pltpu.VMEM((tq, 2 * tq), F32)] * 2 + [pltpu.VMEM((tq, 2 * tq), BF16)] * 2
        + [pltpu.VMEM((1, 2 * tq), F32)] * 5 + [pltpu.VMEM((ATTN_VROWS, 2 * tq), F32)],
        compiler_params=pltpu.CompilerParams(
            dimension_semantics=("arbitrary", "arbitrary", "arbitrary"), vmem_limit_bytes=VMEM_LIMIT),
        name="diffattn",
    )(proj_c, proj_c, vt, jnp.broadcast_to(lam.reshape(1, 1), (1, LANES)).astype(F32),
      subln_g.reshape(C_HEADS, C_VHEAD, 1))


def _s5_params(a_re, a_im, log_dt, b_re, b_im, c_re, c_im, d_skip, w_glu, b_glu, beta, n_levels):
    hp = lax.Precision.HIGHEST
    g, p = a_re.shape
    dt = jnp.exp(log_dt)[:, None]
    mag = jnp.exp(dt * a_re)
    abr, abi = mag * jnp.cos(dt * a_im), mag * jnp.sin(dt * a_im)
    den = a_re * a_re + a_im * a_im
    zr = ((abr - 1.0) * a_re + abi * a_im) / den
    zi = (abi * a_re - (abr - 1.0) * a_im) / den
    bbr = zr[..., None] * b_re - zi[..., None] * b_im
    bbi = zr[..., None] * b_im + zi[..., None] * b_re
    pwr, pwi = jnp.ones((g, 1, p), F32), jnp.zeros((g, 1, p), F32)
    qr, qi = abr[:, None, :], abi[:, None, :]
    n = 1
    while n < S5_LC:
        pwr, pwi = (jnp.concatenate([pwr, pwr * qr - pwi * qi], 1),
                    jnp.concatenate([pwi, pwr * qi + pwi * qr], 1))
        qr, qi = qr * qr - qi * qi, 2.0 * qr * qi
        n *= 2
    lev1, lev2 = [], []
    for _ in range(n_levels):
        lev1.append(jnp.concatenate([qr, qr], -1))
        lev2.append(jnp.concatenate([-qi, qi], -1))
        qr, qi = qr * qr - qi * qi, 2.0 * qr * qi
    lev1 = jnp.concatenate(lev1, 1)
    lev2 = jnp.concatenate(lev2, 1)
    pw1r = jnp.concatenate([pwr[:, 1:], lev1[:, :1, :p]], 1)
    pw1i = jnp.concatenate([pwi[:, 1:], lev2[:, :1, p:]], 1)
    cpr = c_re[:, None] * pwr[:, :, None, :] - c_im[:, None] * pwi[:, :, None, :]
    cpi = c_re[:, None] * pwi[:, :, None, :] + c_im[:, None] * pwr[:, :, None, :]
    kj = (jnp.einsum('gjhp,gpk->gjhk', cpr, bbr, precision=hp)
          - jnp.einsum('gjhp,gpk->gjhk', cpi, bbi, precision=hp))
    kj = kj.at[:, 0].add(d_skip[:, :, None] * jnp.eye(B_GROUP, dtype=F32))
    lag = jnp.arange(S5_LC)[None, :] - jnp.arange(S5_LC)[:, None]
    kj_pad = jnp.concatenate([kj.transpose(0, 3, 1, 2).astype(BF16),
                              jnp.zeros((g, B_GROUP, 1, B_GROUP), BF16)], axis=2)
    kmat = kj_pad[:, :, jnp.where(lag >= 0, lag, S5_LC), :].reshape(g, S5_ROW, S5_ROW)
    rr, ri = pwr[:, ::-1], pwi[:, ::-1]
    bpr = rr[:, :, :, None] * bbr[:, None] - ri[:, :, :, None] * bbi[:, None]
    bpi = rr[:, :, :, None] * bbi[:, None] + ri[:, :, :, None] * bbr[:, None]
    bmat = jnp.concatenate([bpr.transpose(0, 3, 1, 2), bpi.transpose(0, 3, 1, 2)], -1)
    bmat = bmat.reshape(g, S5_ROW, 2 * p)
    cqr = c_re[:, None] * pw1r[:, :, None, :] - c_im[:, None] * pw1i[:, :, None, :]
    cqi = c_re[:, None] * pw1i[:, :, None, :] + c_im[:, None] * pw1r[:, :, None, :]
    cmat = jnp.concatenate([cqr.transpose(0, 3, 1, 2), -cqi.transpose(0, 3, 1, 2)], 1)
    cmat = cmat.reshape(g, 2 * p, S5_ROW)
    rep = LANES // B_GROUP
    eye = jnp.eye(rep, dtype=F32)
    glu = jnp.einsum('ab,ghk->gahbk', eye, w_glu).reshape(g, LANES, LANES)
    bglu = jnp.tile(b_glu, (1, rep)).reshape(g, 1, LANES)
    betat = jnp.tile(beta.reshape(g, B_GROUP), (1, S5_LC)).reshape(g, 1, S5_ROW)
    return (kmat.astype(BF16), bmat.astype(BF16), cmat.astype(BF16), glu.astype(BF16),
            bglu, betat, lev1, lev2)


def _gelu_tanh(x):
    return 0.5 * x * (1.0 + jnp.tanh(math.sqrt(2.0 / math.pi) * (x + 0.044715 * (x * x * x))))


def _s5_kernel(u_ref, kmat_ref, bmat_ref, cmat_ref, glu_ref, bglu_ref, beta_ref, lev1_ref, lev2_ref,
               mean_ref, o_ref, *, rows_per_batch, n_levels):
    u = u_ref[...].astype(BF16)
    y = _dot(u, kmat_ref[...])
    x = _dot(u, bmat_ref[...])
    rpos = lax.broadcasted_iota(jnp.int32, x.shape, 0) & (rows_per_batch - 1)
    for lv in range(n_levels):
        d = 1 << lv
        if d >= rows_per_batch:
            break
        sh = jnp.where(rpos >= d, pltpu.roll(x, d, axis=0), 0.0)
        sw = pltpu.roll(sh, B_STATE, axis=1)
        x = x + lev1_ref[lv:lv + 1, :] * sh + lev2_ref[lv:lv + 1, :] * sw
    xs = jnp.where(rpos >= 1, pltpu.roll(x, 1, axis=0), 0.0)
    y = y + _dot(xs.astype(BF16), cmat_ref[...])
    y = _gelu_tanh(y)
    glu = glu_ref[...]
    mean = mean_ref[...]
    for c in range(S5_ROW // LANES):
        sl = slice(c * LANES, (c + 1) * LANES)
        yc = y[:, sl]
        z = _dot(yc.astype(BF16), glu) + bglu_ref[...]
        yc = yc * jax.nn.sigmoid(z)
        ms = _dot_x3(yc * yc, mean)
        o_ref[:, sl] = yc * lax.rsqrt(ms + RMS_EPS) * beta_ref[:, sl]


def _s5(proj, params, bsz, seq):
    t = proj.shape[0]
    rows = t // S5_LC
    rows_per_batch = seq // S5_LC
    n_levels = max(1, (rows_per_batch - 1).bit_length())
    kmat, bmat, cmat, glu, bglu, betat, lev1, lev2 = params
    u = proj[:, A_IN:A_IN + B_WIDTH].reshape(rows, S5_LC, B_GROUPS, B_GROUP)
    u = u.transpose(2, 0, 3, 1).reshape(B_GROUPS, rows, S5_ROW)
    rep = LANES // B_GROUP
    mean = jnp.kron(jnp.eye(rep, dtype=F32), jnp.full((B_GROUP, B_GROUP), 1.0 / B_GROUP, F32)).astype(BF16)
    per_g = lambda a: pl.BlockSpec((None,) + a.shape[1:], lambda g: (g,) + (0,) * (a.ndim - 1))
    y = pl.pallas_call(
        functools.partial(_s5_kernel, rows_per_batch=rows_per_batch, n_levels=n_levels),
        grid=(B_GROUPS,),
        in_specs=[per_g(u), per_g(kmat), per_g(bmat), per_g(cmat), per_g(glu), per_g(bglu), per_g(betat),
                  per_g(lev1), per_g(lev2), pl.BlockSpec(mean.shape, lambda g: (0, 0))],
        out_specs=pl.BlockSpec((None, rows, S5_ROW), lambda g: (g, 0, 0)),
        out_shape=jax.ShapeDtypeStruct((B_GROUPS, rows, S5_ROW), F32),
        compiler_params=pltpu.CompilerParams(
            dimension_semantics=("arbitrary",), vmem_limit_bytes=VMEM_LIMIT),
        name="s5",
    )(u, kmat, bmat, cmat, glu, bglu, betat, lev1, lev2, mean)
    return y.reshape(B_GROUPS, rows, S5_LC, B_GROUP).transpose(1, 2, 0, 3).reshape(t, B_WIDTH)


def _rwkv_chunks(pairs, strict, incl, first):
    c = CHUNK
    zero = jnp.zeros((), BF16)
    st = lambda x: jnp.concatenate([jnp.where(first, x, zero), jnp.where(first, zero, x)], axis=0)
    cat = lambda xs: jnp.concatenate(xs, axis=0)
    v_st = [st(p[4]) for p in pairs]
    gmat = [_dot_nt(cat([p[0], p[0], p[1], p[1]]), cat([st(p[2]), st(p[3])])) for p in pairs]
    ls = [_dot_nt(cat([st(p[0]), st(p[1])]), p[8].astype(BF16)) for p in pairs]
    a_ab = [jnp.where(strict, g[:2 * c, :2 * c], 0.0).astype(BF16) for g in gmat]
    a_k = [cat([jnp.where(strict, g[:2 * c, 2 * c:], 0.0), jnp.where(incl, g[2 * c:, 2 * c:], 0.0)]).astype(BF16)
           for g in gmat]
    a_rb = [jnp.where(incl, g[2 * c:, :2 * c], 0.0).astype(BF16) for g in gmat]
    av = [_dot(ak, vs) for ak, vs in zip(a_k, v_st)]
    u = [l[:2 * c] + a[:2 * c] for l, a in zip(ls, av)]
    pw = a_ab
    n = 1
    while n < c:
        if 2 * n >= c:
            x = [_dot(p_, u_.astype(BF16)) for p_, u_ in zip(pw, u)]
            u = [u_ + x_ for u_, x_ in zip(u, x)]
        else:
            x = [_dot(p_, jnp.concatenate([u_.astype(BF16), p_], axis=1)) for p_, u_ in zip(pw, u)]
            u = [u_ + x_[:, :2 * c] for u_, x_ in zip(u, x)]
            pw = [x_[:, 2 * c:].astype(BF16) for x_ in x]
        n *= 2
    yst = [l[2 * c:] + a[2 * c:] + _dot(arb, u_.astype(BF16)) for l, a, arb, u_ in zip(ls, av, a_rb, u)]
    ys = [y_[:c] + y_[c:] for y_ in yst]
    s_new = []
    for p, u_, vs in zip(pairs, u, v_st):
        lhs_t = cat([u_, vs.astype(F32)]).T.astype(BF16)
        s_new.append(p[8] * p[7] + _dot(lhs_t, cat([st(p[5]), st(p[6])])))
    return ys, s_new


def _rwkv_kernel(p_ref, mu_ref, w0_ref, a0_ref, wch_ref, wcl_ref, guh_ref, gul_ref, kkp_ref, kap_ref,
                 rk_ref, gng_ref, gnb_ref, seg_ref, tri_ref, o_ref,
                 state_ref, carry_ref, al_s, rt_s, be_s, kh_s, v_s, bc_s, kc_s, gend_s, y_s, bonus_s,
                 gate_s, *, tb):
    ti = pl.program_id(1)

    @pl.when(ti == 0)
    def _():
        state_ref[...] = jnp.zeros_like(state_ref)
        carry_ref[...] = jnp.zeros_like(carry_ref)

    p = p_ref[...]
    row = lax.broadcasted_iota(jnp.int32, p.shape, 0)
    prev = jnp.where(row == 0, carry_ref[0:1, :], pltpu.roll(p, 1, axis=0))
    carry_ref[0:1, :] = p[tb - 1:tb, :]
    xs = p + (prev - p) * mu_ref[...]
    r = xs[:, 0:A_WIDTH]
    k = xs[:, A_WIDTH:2 * A_WIDTH]
    v = xs[:, 2 * A_WIDTH:3 * A_WIDTH]
    z = xs[:, 3 * A_WIDTH:3 * A_WIDTH + LANES]
    lane = lax.broadcasted_iota(jnp.int32, z.shape, 1)
    z = jnp.where(lane < A_DECAY_LORA, jnp.tanh(z), z)
    wa = _dot_hl(z, wch_ref[...], wcl_ref[...])
    w = -jax.nn.softplus(-(w0_ref[...] + wa[:, :A_WIDTH])) - 0.5
    lw = -jnp.exp(w)
    a = jax.nn.sigmoid(a0_ref[...] + wa[:, A_WIDTH:])
    gate = _dot_hl(jax.nn.sigmoid(xs[:, 3 * A_WIDTH + LANES:]), guh_ref[...], gul_ref[...])
    kk = k * kkp_ref[...]
    k = k * (1.0 + (a - 1.0) * kap_ref[...])
    seg = seg_ref[...]
    blocks = [slice(j * LANES, (j + 1) * LANES) for j in range(A_PAIRS)]
    segsum = lambda x: jnp.concatenate([_dot_x2(x[:, b], seg) for b in blocks], axis=1)
    kk = kk * lax.rsqrt(segsum(kk * kk) + 1e-12)
    gate_s[...] = gate
    bonus_s[...] = segsum(r * k * rk_ref[...]) * v * gate
    v_s[...] = v.astype(BF16)
    tri = tri_ref[...]
    for c in range(tb // CHUNK):
        rows = slice(c * CHUNK, (c + 1) * CHUNK)
        lwc = lw[rows, :]
        cum = _dot_x3_left(tri, lwc)
        g = jnp.exp(cum)
        gi = jnp.exp(-cum)
        g_end = g[CHUNK - 1:CHUNK, :]
        be = kk[rows, :] * a[rows, :] * gi
        kh = k[rows, :] * gi
        al_s[rows, :] = (-kk[rows, :] * jnp.exp(cum - lwc)).astype(BF16)
        rt_s[rows, :] = (r[rows, :] * g).astype(BF16)
        be_s[rows, :] = be.astype(BF16)
        kh_s[rows, :] = kh.astype(BF16)
        bc_s[rows, :] = (be * g_end).astype(BF16)
        kc_s[rows, :] = (kh * g_end).astype(BF16)
        gend_s[8 * c:8 * c + 1, :] = g_end

    i2 = lax.broadcasted_iota(jnp.int32, (2 * CHUNK, 2 * CHUNK), 0)
    j2 = lax.broadcasted_iota(jnp.int32, (2 * CHUNK, 2 * CHUNK), 1)
    same = (i2 >> CHUNK_SHIFT) == (j2 >> CHUNK_SHIFT)
    strict = same & ((j2 & (CHUNK - 1)) < (i2 & (CHUNK - 1)))
    incl = same & ((j2 & (CHUNK - 1)) <= (i2 & (CHUNK - 1)))
    first = lax.broadcasted_iota(jnp.int32, (CHUNK, LANES), 1) < A_HEAD

    def chunk_body(c, carry):
        rows = pl.ds(pl.multiple_of(c * CHUNK, CHUNK), CHUNK)
        grow = pl.ds(pl.multiple_of(c * 8, 8), 1)
        pairs = [(al_s[rows, b], rt_s[rows, b], be_s[rows, b], kh_s[rows, b], v_s[rows, b], bc_s[rows, b],
                  kc_s[rows, b], gend_s[grow, b], state_ref[j]) for j, b in enumerate(blocks)]
        ys, s_new = _rwkv_chunks(pairs, strict, incl, first)
        for j, b in enumerate(blocks):
            y_s[rows, b] = ys[j]
            state_ref[j] = s_new[j]
        return carry

    lax.fori_loop(0, tb // CHUNK, chunk_body, 0)

    y = y_s[...]
    inv = 1.0 / A_HEAD
    mean = segsum(y) * inv
    yc = y - mean
    var = segsum(yc * yc) * inv
    y = yc * lax.rsqrt(var + RWKV_GN_EPS) * gng_ref[...] + gnb_ref[...]
    o_ref[...] = y * gate_s[...] + bonus_s[...]


def _rwkv(proj, prm, bsz, seq, tb):
    t = proj.shape[0]
    nt = seq // tb
    (mu, w0, a0, wch, wcl, guh, gul, kkp, kap, rk, gng, gnb) = prm
    seg = jnp.kron(jnp.eye(2, dtype=F32), jnp.ones((A_HEAD, A_HEAD), F32)).astype(BF16)
    tri = jnp.tril(jnp.ones((CHUNK, CHUNK), F32)).astype(BF16)
    full = lambda a_: pl.BlockSpec(a_.shape, lambda b, i: (0,) * a_.ndim)
    vm = lambda dt: pltpu.VMEM((tb, A_WIDTH), dt)
    return pl.pallas_call(
        functools.partial(_rwkv_kernel, tb=tb),
        grid=(bsz, nt),
        in_specs=[pl.BlockSpec((tb, A_IN), lambda b, i: (b * nt + i, 0))]
        + [full(x) for x in (mu, w0, a0, wch, wcl, guh, gul, kkp, kap, rk, gng, gnb, seg, tri)],
        out_specs=pl.BlockSpec((tb, A_WIDTH), lambda b, i: (b * nt + i, 0)),
        out_shape=jax.ShapeDtypeStruct((t, A_WIDTH), F32),
        scratch_shapes=[pltpu.VMEM((A_PAIRS, LANES, LANES), F32), pltpu.VMEM((8, A_IN), F32)]
        + [vm(BF16) for _ in range(7)] + [pltpu.VMEM((tb // CHUNK * 8, A_WIDTH), F32)]
        + [vm(F32) for _ in range(3)],
        compiler_params=pltpu.CompilerParams(
            dimension_semantics=("arbitrary", "arbitrary"), vmem_limit_bytes=VMEM_LIMIT),
        name="rwkv7",
    )(proj, mu, w0, a0, wch, wcl, guh, gul, kkp, kap, rk, gng, gnb, seg, tri)


def _rwkv_params(mu, w0, w_up, a0, a_up, g_up, k_k, k_a, r_k, gn_g, gn_b):
    row = lambda x: x.reshape(1, -1).astype(F32)
    zeros = jnp.zeros_like(w_up)
    wc = jnp.concatenate([jnp.concatenate([w_up, zeros], 1), jnp.concatenate([zeros, a_up], 1)], 0)
    hl = lambda m: (m.astype(BF16), (m - m.astype(BF16).astype(F32)).astype(BF16))
    wch, wcl = hl(wc)
    guh, gul = hl(g_up)
    return (row(mu), row(w0), row(a0), wch, wcl, guh, gul, row(k_k), row(k_a), row(r_k), row(gn_g),
            row(gn_b))


def _pick(n, pref):
    t = min(n, pref)
    while n % t:
        t //= 2
    return t


def kernel(x, norm_mix_g, w_in, rwkv_mu, rwkv_w0, rwkv_w_up, rwkv_a0, rwkv_a_up, rwkv_g_up, rwkv_k_k, rwkv_k_a, rwkv_r_k, rwkv_gn_g, rwkv_gn_b, s5_a_re, s5_a_im, s5_log_dt, s5_b_re, s5_b_im, s5_c_re, s5_c_im, s5_d, s5_w_glu, s5_b_glu, s5_beta, diff_lq1, diff_lk1, diff_lq2, diff_lk2, diff_subln_g, w_out, norm_ff_g, w_ff1, w_ff2, norm_final_g):
    bsz, seq, d = x.shape
    t = bsz * seq
    depth = w_in.shape[0]
    x2 = x.reshape(t, d).astype(F32)
    tm = _pick(t, 512)
    tm_in = _pick(t, 1024)
    tq = _pick(seq, 512)
    tb = _pick(seq, 256)
    assert tq & (tq - 1) == 0 and (seq // S5_LC) & (seq // S5_LC - 1) == 0 and tb % CHUNK == 0
    rows_per_batch = seq // S5_LC
    n_levels = max(1, (rows_per_batch - 1).bit_length())
    for l in range(depth):
        proj, proj_c = _inproj(x2, norm_mix_g[l], w_in[l].astype(BF16), tm_in, 768, A_IN + B_WIDTH)
        prm = _rwkv_params(rwkv_mu[l], rwkv_w0[l], rwkv_w_up[l], rwkv_a0[l], rwkv_a_up[l], rwkv_g_up[l],
                           rwkv_k_k[l], rwkv_k_a[l], rwkv_r_k[l].reshape(-1), rwkv_gn_g[l], rwkv_gn_b[l])
        y_a = _rwkv(proj, prm, bsz, seq, tb)
        s5p = _s5_params(s5_a_re[l], s5_a_im[l], s5_log_dt[l], s5_b_re[l], s5_b_im[l], s5_c_re[l],
                         s5_c_im[l], s5_d[l], s5_w_glu[l], s5_b_glu[l], s5_beta[l], n_levels)
        y_b = _s5(proj, s5p, bsz, seq)
        lambda_init = 0.8 - 0.6 * math.exp(-0.3 * l)
        lam = (jnp.exp(jnp.sum(diff_lq1[l] * diff_lk1[l])) - jnp.exp(jnp.sum(diff_lq2[l] * diff_lk2[l]))
               + lambda_init)
        y_c = _attn(proj_c, lam, diff_subln_g[l], bsz, seq, lambda_init, tq)
        x2 = _outproj(x2, y_a, y_b, y_c, w_out[l].astype(BF16), tm)
        x2 = _ffn(x2, norm_ff_g[l], w_ff1[l].astype(BF16), w_ff2[l].astype(BF16), norm_final_g,
                  l == depth - 1, tm, 1024)
    return x2.reshape(bsz, seq, d)
```

```python
import functools
import math

import jax
import jax.numpy as jnp
import numpy as np
from jax import lax
from jax.experimental import pallas as pl
from jax.experimental.pallas import tpu as pltpu

F32 = jnp.float32
BF16 = jnp.bfloat16

D_MODEL = 2048
DEPTH = 2
CHUNK = 64
CHUNK_SHIFT = 6
A_WIDTH = 768
A_HEAD = 64
A_PAIRS = A_WIDTH // (2 * A_HEAD)
A_DECAY_LORA = 64
A_ICLR_LORA = 64
A_GATE_LORA = 128
A_IN = 3 * A_WIDTH + A_DECAY_LORA + A_ICLR_LORA + A_GATE_LORA
B_WIDTH = 512
B_GROUP = 16
B_GROUPS = B_WIDTH // B_GROUP
B_STATE = 64
C_WIDTH = 768
C_HEAD = 64
C_VHEAD = 2 * C_HEAD
C_HEADS = C_WIDTH // C_VHEAD
C_IN = 3 * C_WIDTH
N_IN = A_IN + B_WIDTH + C_IN
D_FF = 4 * D_MODEL
RMS_EPS = 1e-6
SUBLN_EPS = 1e-5
RWKV_GN_EPS = 64e-5
NEG_INF = -1e30
LOG2E = 1.4426950408889634

LANES = 128
VMEM_LIMIT = 56 * 1024 * 1024
ATTN_CG = 256
ATTN_VROWS = C_VHEAD + 16
S5_LC = 64
S5_ROW = S5_LC * B_GROUP


def _dot(a, b):
    return jnp.dot(a, b, preferred_element_type=F32)


def _dot_nt(a, b):
    return lax.dot_general(a, b, (((1,), (1,)), ((), ())), preferred_element_type=F32)


def _split3(x):
    hi = x.astype(BF16)
    r1 = x - hi.astype(F32)
    mid = r1.astype(BF16)
    lo = (r1 - mid.astype(F32)).astype(BF16)
    return hi, mid, lo


def _dot_x3(x, m):
    hi, mid, lo = _split3(x)
    return _dot(hi, m) + _dot(mid, m) + _dot(lo, m)


def _dot_x2(x, m):
    hi = x.astype(BF16)
    mid = (x - hi.astype(F32)).astype(BF16)
    return _dot(hi, m) + _dot(mid, m)


def _dot_x3_left(m, x):
    hi, mid, lo = _split3(x)
    return _dot(m, hi) + _dot(m, mid) + _dot(m, lo)


def _dot_hl(x, m_hi, m_lo):
    hi, mid, _ = _split3(x)
    return _dot(hi, m_hi) + _dot(mid, m_hi) + _dot(hi, m_lo)


def _rms(x, g, eps):
    return x * lax.rsqrt(jnp.mean(x * x, axis=-1, keepdims=True) + eps) * g


def _inproj_kernel(x_ref, g_ref, w_ref, o32_ref, o16_ref, h_ref, *, n32):
    j = pl.program_id(1)

    @pl.when(j == 0)
    def _():
        h_ref[...] = _rms(x_ref[...], g_ref[...], RMS_EPS).astype(BF16)

    @pl.when(j < n32)
    def _():
        o32_ref[...] = _dot(h_ref[...], w_ref[...])

    @pl.when(j >= n32)
    def _():
        o16_ref[...] = _dot(h_ref[...], w_ref[...]).astype(BF16)


def _inproj(x2, g, w_bf, tm, tn, n_f32):
    t, d = x2.shape
    n = w_bf.shape[1]
    n32 = n_f32 // tn
    return pl.pallas_call(
        functools.partial(_inproj_kernel, n32=n32),
        grid=(t // tm, n // tn),
        in_specs=[pl.BlockSpec((tm, d), lambda i, j: (i, 0)),
                  pl.BlockSpec((1, d), lambda i, j: (0, 0)),
                  pl.BlockSpec((d, tn), lambda i, j: (0, j))],
        out_specs=[pl.BlockSpec((tm, tn), lambda i, j: (i, jnp.minimum(j, n32 - 1))),
                   pl.BlockSpec((tm, tn), lambda i, j: (i, jnp.maximum(j - n32, 0)))],
        out_shape=[jax.ShapeDtypeStruct((t, n_f32), F32), jax.ShapeDtypeStruct((t, n - n_f32), BF16)],
        scratch_shapes=[pltpu.VMEM((tm, d), BF16)],
        compiler_params=pltpu.CompilerParams(
            dimension_semantics=("arbitrary", "arbitrary"), vmem_limit_bytes=VMEM_LIMIT),
        name="inproj",
    )(x2, g.reshape(1, d), w_bf)


def _outproj_kernel(x_ref, ya_ref, yb_ref, yc_ref, w_ref, o_ref):
    b0, c0 = A_WIDTH, A_WIDTH + B_WIDTH
    acc = _dot(ya_ref[...].astype(BF16), w_ref[0:b0, :])
    acc += _dot(yb_ref[...].astype(BF16), w_ref[b0:c0, :])
    acc += _dot(yc_ref[...].astype(BF16), w_ref[c0:, :])
    o_ref[...] = x_ref[...] + acc


def _outproj(x2, ya, yb, yc, w_bf, tm):
    t, d = x2.shape
    row = lambda w: pl.BlockSpec((tm, w), lambda i: (i, 0))
    return pl.pallas_call(
        _outproj_kernel,
        grid=(t // tm,),
        in_specs=[row(d), row(A_WIDTH), row(B_WIDTH), row(C_WIDTH),
                  pl.BlockSpec(w_bf.shape, lambda i: (0, 0))],
        out_specs=row(d),
        out_shape=jax.ShapeDtypeStruct((t, d), F32),
        compiler_params=pltpu.CompilerParams(
            dimension_semantics=("arbitrary",), vmem_limit_bytes=VMEM_LIMIT),
        name="outproj",
    )(x2, ya, yb, yc, w_bf)


def _ffn_kernel(x_ref, g_ref, w1_ref, w2_ref, gf_ref, o_ref, h_ref, acc_ref, *, final_norm):
    f = pl.program_id(1)

    @pl.when(f == 0)
    def _():
        h_ref[...] = _rms(x_ref[...], g_ref[...], RMS_EPS).astype(BF16)
        acc_ref[...] = jnp.zeros_like(acc_ref)

    a = _dot(h_ref[...], w1_ref[...])
    a = jnp.square(jnp.maximum(a, 0.0))
    acc_ref[...] += _dot(a.astype(BF16), w2_ref[...])

    @pl.when(f == pl.num_programs(1) - 1)
    def _():
        y = x_ref[...] + acc_ref[...]
        if final_norm:
            y = _rms(y, gf_ref[...], RMS_EPS)
        o_ref[...] = y


def _ffn(x2, g, w1_bf, w2_bf, gf, final_norm, tm, tf):
    t, d = x2.shape
    dff = w1_bf.shape[1]
    return pl.pallas_call(
        functools.partial(_ffn_kernel, final_norm=final_norm),
        grid=(t // tm, dff // tf),
        in_specs=[pl.BlockSpec((tm, d), lambda i, j: (i, 0)),
                  pl.BlockSpec((1, d), lambda i, j: (0, 0)),
                  pl.BlockSpec((d, tf), lambda i, j: (0, j)),
                  pl.BlockSpec((tf, d), lambda i, j: (j, 0)),
                  pl.BlockSpec((1, d), lambda i, j: (0, 0))],
        out_specs=pl.BlockSpec((tm, d), lambda i, j: (i, 0)),
        out_shape=jax.ShapeDtypeStruct((t, d), F32),
        scratch_shapes=[pltpu.VMEM((tm, d), BF16), pltpu.VMEM((tm, d), F32)],
        compiler_params=pltpu.CompilerParams(
            dimension_semantics=("arbitrary", "arbitrary"), vmem_limit_bytes=VMEM_LIMIT),
        name="ffn",
    )(x2, g.reshape(1, d), w1_bf, w2_bf, gf.reshape(1, d))


def _attn_kernel(q_ref, k_ref, vt_ref, lam_ref, g_ref, o_ref,
                 qst_ref, st0_ref, st1_ref, pt0_ref, pt1_ref, mx0_ref, mx1_ref, al0_ref, al1_ref,
                 m_ref, acc_ref, *, tq, out_scale):
    qi = pl.program_id(2)
    st, pt, mx, al = (st0_ref, st1_ref), (pt0_ref, pt1_ref), (mx0_ref, mx1_ref), (al0_ref, al1_ref)
    qt = q_ref[...].astype(F32).T * (C_HEAD ** -0.5 * LOG2E)
    first = lax.broadcasted_iota(jnp.int32, qt.shape, 0) < C_HEAD
    qst_ref[:, 0:tq] = jnp.where(first, qt, 0.0).astype(BF16)
    qst_ref[:, tq:] = jnp.where(first, 0.0, qt).astype(BF16)
    m_ref[...] = jnp.full_like(m_ref, NEG_INF)
    acc_ref[...] = jnp.zeros_like(acc_ref)
    pt1_ref[...] = jnp.zeros_like(pt1_ref)
    al1_ref[...] = jnp.ones_like(al1_ref)
    groups = [slice(g * ATTN_CG, (g + 1) * ATTN_CG) for g in range(2 * tq // ATTN_CG)]

    def scores(tile, slot, masked, cg):
        k = k_ref[pl.ds(pl.multiple_of(tile * tq, tq), tq), :]
        s = _dot(k, qst_ref[:, cg])
        if masked:
            kpos = lax.broadcasted_iota(jnp.int32, s.shape, 0)
            qpos = (lax.broadcasted_iota(jnp.int32, s.shape, 1) + cg.start) & (tq - 1)
            s = jnp.where((kpos >> CHUNK_SHIFT) <= (qpos >> CHUNK_SHIFT), s, NEG_INF)
        st[slot][:, cg] = s
        mx[slot][:, cg] = jnp.max(s, axis=0, keepdims=True)

    def softmax(slot, cg):
        m_prev = m_ref[:, cg]
        m_new = jnp.maximum(m_prev, mx[slot][:, cg])
        m_ref[:, cg] = m_new
        al[slot][:, cg] = jnp.exp2(m_prev - m_new)
        for c0 in range(cg.start, cg.stop, LANES):
            cols = slice(c0, c0 + LANES)
            m_col = m_new[:, c0 - cg.start:c0 - cg.start + LANES]
            for r0 in range(0, tq, LANES):
                p = jnp.exp2(st[slot][r0:r0 + LANES, cols] - m_col)
                pt[slot][r0:r0 + LANES, cols] = p.astype(BF16)

    def accumulate(tile, slot, cg):
        acc_ref[:, cg] = acc_ref[:, cg] * al[slot][:, cg] + _dot(vt_ref[tile], pt[slot][:, cg])

    def step(n, slot):
        prev_tile = jnp.where(n <= 1, qi, n - 2)
        for cg in groups:
            scores(n, 1 - slot, False, cg)
            softmax(slot, cg)
            accumulate(prev_tile, 1 - slot, cg)

    for cg in groups:
        scores(qi, 0, True, cg)

    def body(i, carry):
        step(2 * i, 0)
        step(2 * i + 1, 1)
        return carry

    lax.fori_loop(0, qi >> 1, body, 0)
    tile_before_last = jnp.where(qi <= 1, qi, qi - 2)
    tile_last = jnp.where(qi == 0, qi, qi - 1)

    @pl.when((qi & 1) == 0)
    def _():
        for cg in groups:
            softmax(0, cg)
            accumulate(tile_before_last, 1, cg)
            accumulate(tile_last, 0, cg)

    @pl.when((qi & 1) == 1)
    def _():
        step(qi - 1, 0)
        for cg in groups:
            softmax(1, cg)
            accumulate(tile_before_last, 0, cg)
            accumulate(tile_last, 1, cg)

    o = acc_ref[0:C_VHEAD, :] * (1.0 / acc_ref[C_VHEAD:C_VHEAD + 1, :])
    o = o[:, 0:tq] - lam_ref[0:1, 0:1] * o[:, tq:]
    ms = jnp.mean(o * o, axis=0, keepdims=True)
    o = o * lax.rsqrt(ms + SUBLN_EPS) * (g_ref[...] * out_scale)
    o_ref[...] = o.T.astype(o_ref.dtype)


def _attn(proj_c, lam, subln_g, bsz, seq, lambda_init, tq):
    t = proj_c.shape[0]
    nq = seq // tq
    koff = C_WIDTH // LANES
    v = proj_c[:, 2 * C_WIDTH:].reshape(bsz, nq, tq, C_HEADS, C_VHEAD)
    vt = v.transpose(0, 3, 1, 4, 2)
    ones_rows = jnp.zeros((bsz, C_HEADS, nq, ATTN_VROWS - C_VHEAD, tq), BF16).at[:, :, :, 0, :].set(1.0)
    vt = jnp.concatenate([vt, ones_rows], axis=3)
    return pl.pallas_call(
        functools.partial(_attn_kernel, tq=tq, out_scale=1.0 - lambda_init),
        grid=(bsz, C_HEADS, nq),
        in_specs=[pl.BlockSpec((tq, LANES), lambda b, h, i: (b * nq + i, h)),
                  pl.BlockSpec((seq, LANES), lambda b, h, i: (b, koff + h)),
                  pl.BlockSpec((None, None, nq, ATTN_VROWS, tq), lambda b, h, i: (b, h, 0, 0, 0)),
                  pl.BlockSpec((1, LANES), lambda b, h, i: (0, 0)),
                  pl.BlockSpec((None, C_VHEAD, 1), lambda b, h, i: (h, 0, 0))],
        out_specs=pl.BlockSpec((tq, LANES), lambda b, h, i: (b * nq + i, h)),
        out_shape=jax.ShapeDtypeStruct((t, C_WIDTH), BF16),
        scratch_shapes=[pltpu.VMEM((C_VHEAD, 2 * tq), BF16)]
        + [pltpu.VMEM((tq, 2 * tq), F32)] * 2 + [pltpu.VMEM((tq, 2 * tq), BF16)] * 2
        + [pltpu.VMEM((1, 2 * tq), F32)] * 5 + [pltpu.VMEM((ATTN_VROWS, 2 * tq), F32)],
        compiler_params=pltpu.CompilerParams(
            dimension_semantics=("arbitrary", "arbitrary", "arbitrary"), vmem_limit_bytes=VMEM_LIMIT),
        name="diffattn",
    )(proj_c, proj_c, vt, jnp.broadcast_to(lam.reshape(1, 1), (1, LANES)).astype(F32),
      subln_g.reshape(C_HEADS, C_VHEAD, 1))


def _s5_params(a_re, a_im, log_dt, b_re, b_im, c_re, c_im, d_skip, w_glu, b_glu, beta, n_levels):
    hp = lax.Precision.HIGHEST
    g, p = a_re.shape
    dt = jnp.exp(log_dt)[:, None]
    mag = jnp.exp(dt * a_re)
    abr, abi = mag * jnp.cos(dt * a_im), mag * jnp.sin(dt * a_im)
    den = a_re * a_re + a_im * a_im
    zr = ((abr - 1.0) * a_re + abi * a_im) / den
    zi = (abi * a_re - (abr - 1.0) * a_im) / den
    bbr = zr[..., None] * b_re - zi[..., None] * b_im
    bbi = zr[..., None] * b_im + zi[..., None] * b_re
    pwr, pwi = jnp.ones((g, 1, p), F32), jnp.zeros((g, 1, p), F32)
    qr, qi = abr[:, None, :], abi[:, None, :]
    n = 1
    while n < S5_LC:
        pwr, pwi = (jnp.concatenate([pwr, pwr * qr - pwi * qi], 1),
                    jnp.concatenate([pwi, pwr * qi + pwi * qr], 1))
        qr, qi = qr * qr - qi * qi, 2.0 * qr * qi
        n *= 2
    lev1, lev2 = [], []
    for _ in range(n_levels):
        lev1.append(jnp.concatenate([qr, qr], -1))
        lev2.append(jnp.concatenate([-qi, qi], -1))
        qr, qi = qr * qr - qi * qi, 2.0 * qr * qi
    lev1 = jnp.concatenate(lev1, 1)
    lev2 = jnp.concatenate(lev2, 1)
    pw1r = jnp.concatenate([pwr[:, 1:], lev1[:, :1, :p]], 1)
    pw1i = jnp.concatenate([pwi[:, 1:], lev2[:, :1, p:]], 1)
    cpr = c_re[:, None] * pwr[:, :, None, :] - c_im[:, None] * pwi[:, :, None, :]
    cpi = c_re[:, None] * pwi[:, :, None, :] + c_im[:, None] * pwr[:, :, None, :]
    kj = (jnp.einsum('gjhp,gpk->gjhk', cpr, bbr, precision=hp)
          - jnp.einsum('gjhp,gpk->gjhk', cpi, bbi, precision=hp))
    kj = kj.at[:, 0].add(d_skip[:, :, None] * jnp.eye(B_GROUP, dtype=F32))
    lag = jnp.arange(S5_LC)[None, :] - jnp.arange(S5_LC)[:, None]
    kj_pad = jnp.concatenate([kj.transpose(0, 3, 1, 2).astype(BF16),
                              jnp.zeros((g, B_GROUP, 1, B_GROUP), BF16)], axis=2)
    kmat = kj_pad[:, :, jnp.where(lag >= 0, lag, S5_LC), :].reshape(g, S5_ROW, S5_ROW)
    rr, ri = pwr[:, ::-1], pwi[:, ::-1]
    bpr = rr[:, :, :, None] * bbr[:, None] - ri[:, :, :, None] * bbi[:, None]
    bpi = rr[:, :, :, None] * bbi[:, None] + ri[:, :, :, None] * bbr[:, None]
    bmat = jnp.concatenate([bpr.transpose(0, 3, 1, 2), bpi.transpose(0, 3, 1, 2)], -1)
    bmat = bmat.reshape(g, S5_ROW, 2 * p)
    cqr = c_re[:, None] * pw1r[:, :, None, :] - c_im[:, None] * pw1i[:, :, None, :]
    cqi = c_re[:, None] * pw1i[:, :, None, :] + c_im[:, None] * pw1r[:, :, None, :]
    cmat = jnp.concatenate([cqr.transpose(0, 3, 1, 2), -cqi.transpose(0, 3, 1, 2)], 1)
    cmat = cmat.reshape(g, 2 * p, S5_ROW)
    rep = LANES // B_GROUP
    eye = jnp.eye(rep, dtype=F32)
    glu = jnp.einsum('ab,ghk->gahbk', eye, w_glu).reshape(g, LANES, LANES)
    bglu = jnp.tile(b_glu, (1, rep)).reshape(g, 1, LANES)
    betat = jnp.tile(beta.reshape(g, B_GROUP), (1, S5_LC)).reshape(g, 1, S5_ROW)
    return (kmat.astype(BF16), bmat.astype(BF16), cmat.astype(BF16), glu.astype(BF16),
            bglu, betat, lev1, lev2)


def _gelu_tanh(x):
    return 0.5 * x * (1.0 + jnp.tanh(math.sqrt(2.0 / math.pi) * (x + 0.044715 * (x * x * x))))


def _s5_kernel(u_ref, kmat_ref, bmat_ref, cmat_ref, glu_ref, bglu_ref, beta_ref, lev1_ref, lev2_ref,
               mean_ref, o_ref, *, rows_per_batch, n_levels):
    u = u_ref[...]
    y = _dot(u, kmat_ref[...])
    x = _dot(u, bmat_ref[...])
    rpos = lax.broadcasted_iota(jnp.int32, x.shape, 0) & (rows_per_batch - 1)
    for lv in range(n_levels):
        d = 1 << lv
        if d >= rows_per_batch:
            break
        sh = jnp.where(rpos >= d, pltpu.roll(x, d, axis=0), 0.0)
        sw = pltpu.roll(sh, B_STATE, axis=1)
        x = x + lev1_ref[lv:lv + 1, :] * sh + lev2_ref[lv:lv + 1, :] * sw
    xs = jnp.where(rpos >= 1, pltpu.roll(x, 1, axis=0), 0.0)
    y = y + _dot(xs.astype(BF16), cmat_ref[...])
    y = _gelu_tanh(y)
    glu = glu_ref[...]
    mean = mean_ref[...]
    for c in range(S5_ROW // LANES):
        sl = slice(c * LANES, (c + 1) * LANES)
        yc = y[:, sl]
        z = _dot(yc.astype(BF16), glu) + bglu_ref[...]
        yc = yc * jax.nn.sigmoid(z)
        ms = _dot_x3(yc * yc, mean)
        o_ref[:, sl] = (yc * lax.rsqrt(ms + RMS_EPS) * beta_ref[:, sl]).astype(o_ref.dtype)


def _s5(proj, params, bsz, seq):
    t = proj.shape[0]
    rows = t // S5_LC
    rows_per_batch = seq // S5_LC
    n_levels = max(1, (rows_per_batch - 1).bit_length())
    kmat, bmat, cmat, glu, bglu, betat, lev1, lev2 = params
    u = proj[:, A_IN:A_IN + B_WIDTH].astype(BF16).reshape(rows, S5_LC, B_GROUPS, B_GROUP)
    u = u.transpose(2, 0, 3, 1).reshape(B_GROUPS, rows, S5_ROW)
    rep = LANES // B_GROUP
    mean = jnp.kron(jnp.eye(rep, dtype=F32), jnp.full((B_GROUP, B_GROUP), 1.0 / B_GROUP, F32)).astype(BF16)
    per_g = lambda a: pl.BlockSpec((None,) + a.shape[1:], lambda g: (g,) + (0,) * (a.ndim - 1))
    y = pl.pallas_call(
        functools.partial(_s5_kernel, rows_per_batch=rows_per_batch, n_levels=n_levels),
        grid=(B_GROUPS,),
        in_specs=[per_g(u), per_g(kmat), per_g(bmat), per_g(cmat), per_g(glu), per_g(bglu), per_g(betat),
                  per_g(lev1), per_g(lev2), pl.BlockSpec(mean.shape, lambda g: (0, 0))],
        out_specs=pl.BlockSpec((None, rows, S5_ROW), lambda g: (g, 0, 0)),
        out_shape=jax.ShapeDtypeStruct((B_GROUPS, rows, S5_ROW), BF16),
        compiler_params=pltpu.CompilerParams(
            dimension_semantics=("arbitrary",), vmem_limit_bytes=VMEM_LIMIT),
        name="s5",
    )(u, kmat, bmat, cmat, glu, bglu, betat, lev1, lev2, mean)
    return y.reshape(B_GROUPS, rows, S5_LC, B_GROUP).transpose(1, 2, 0, 3).reshape(t, B_WIDTH)


def _rwkv_chunks(pairs, strict, incl, first):
    c = CHUNK
    zero = jnp.zeros((), BF16)
    st = lambda x: jnp.concatenate([jnp.where(first, x, zero), jnp.where(first, zero, x)], axis=0)
    cat = lambda xs: jnp.concatenate(xs, axis=0)
    v_st = [st(p[4]) for p in pairs]
    gmat = [_dot_nt(cat([p[0], p[0], p[1], p[1]]), cat([st(p[2]), st(p[3])])) for p in pairs]
    ls = [_dot_nt(cat([st(p[0]), st(p[1])]), p[8].astype(BF16)) for p in pairs]
    a_ab = [jnp.where(strict, g[:2 * c, :2 * c], 0.0).astype(BF16) for g in gmat]
    a_k = [cat([jnp.where(strict, g[:2 * c, 2 * c:], 0.0), jnp.where(incl, g[2 * c:, 2 * c:], 0.0)]).astype(BF16)
           for g in gmat]
    a_rb = [jnp.where(incl, g[2 * c:, :2 * c], 0.0).astype(BF16) for g in gmat]
    av = [_dot(ak, vs) for ak, vs in zip(a_k, v_st)]
    u = [l[:2 * c] + a[:2 * c] for l, a in zip(ls, av)]
    pw = a_ab
    n = 1
    while n < c:
        if 2 * n >= c:
            x = [_dot(p_, u_.astype(BF16)) for p_, u_ in zip(pw, u)]
            u = [u_ + x_ for u_, x_ in zip(u, x)]
        else:
            x = [_dot(p_, jnp.concatenate([u_.astype(BF16), p_], axis=1)) for p_, u_ in zip(pw, u)]
            u = [u_ + x_[:, :2 * c] for u_, x_ in zip(u, x)]
            pw = [x_[:, 2 * c:].astype(BF16) for x_ in x]
        n *= 2
    yst = [l[2 * c:] + a[2 * c:] + _dot(arb, u_.astype(BF16)) for l, a, arb, u_ in zip(ls, av, a_rb, u)]
    ys = [y_[:c] + y_[c:] for y_ in yst]
    s_new = []
    for p, u_, vs in zip(pairs, u, v_st):
        lhs_t = cat([u_, vs.astype(F32)]).T.astype(BF16)
        s_new.append(p[8] * p[7] + _dot(lhs_t, cat([st(p[5]), st(p[6])])))
    return ys, s_new


def _rwkv_kernel(p_ref, mu_ref, w0_ref, a0_ref, wch_ref, wcl_ref, guh_ref, gul_ref, kkp_ref, kap_ref,
                 rk_ref, gng_ref, gnb_ref, seg_ref, tri_ref, o_ref,
                 state_ref, carry_ref, al_s, rt_s, be_s, kh_s, v_s, bc_s, kc_s, gend_s, y_s, bonus_s,
                 gate_s, *, tb):
    ti = pl.program_id(1)

    @pl.when(ti == 0)
    def _():
        state_ref[...] = jnp.zeros_like(state_ref)
        carry_ref[...] = jnp.zeros_like(carry_ref)

    p = p_ref[...]
    row = lax.broadcasted_iota(jnp.int32, p.shape, 0)
    prev = jnp.where(row == 0, carry_ref[0:1, :], pltpu.roll(p, 1, axis=0))
    carry_ref[0:1, :] = p[tb - 1:tb, :]
    xs = p + (prev - p) * mu_ref[...]
    r = xs[:, 0:A_WIDTH]
    k = xs[:, A_WIDTH:2 * A_WIDTH]
    v = xs[:, 2 * A_WIDTH:3 * A_WIDTH]
    z = xs[:, 3 * A_WIDTH:3 * A_WIDTH + LANES]
    lane = lax.broadcasted_iota(jnp.int32, z.shape, 1)
    z = jnp.where(lane < A_DECAY_LORA, jnp.tanh(z), z)
    wa = _dot_hl(z, wch_ref[...], wcl_ref[...])
    w = -jax.nn.softplus(-(w0_ref[...] + wa[:, :A_WIDTH])) - 0.5
    lw = -jnp.exp(w)
    a = jax.nn.sigmoid(a0_ref[...] + wa[:, A_WIDTH:])
    gate = _dot_hl(jax.nn.sigmoid(xs[:, 3 * A_WIDTH + LANES:]), guh_ref[...], gul_ref[...])
    kk = k * kkp_ref[...]
    k = k * (1.0 + (a - 1.0) * kap_ref[...])
    seg = seg_ref[...]
    blocks = [slice(j * LANES, (j + 1) * LANES) for j in range(A_PAIRS)]
    segsum = lambda x: jnp.concatenate([_dot_x2(x[:, b], seg) for b in blocks], axis=1)
    kk = kk * lax.rsqrt(segsum(kk * kk) + 1e-12)
    gate_s[...] = gate
    bonus_s[...] = segsum(r * k * rk_ref[...]) * v * gate
    v_s[...] = v.astype(BF16)
    tri = tri_ref[...]
    for c in range(tb // CHUNK):
        rows = slice(c * CHUNK, (c + 1) * CHUNK)
        lwc = lw[rows, :]
        cum = _dot_x3_left(tri, lwc)
        g = jnp.exp(cum)
        gi = jnp.exp(-cum)
        g_end = g[CHUNK - 1:CHUNK, :]
        be = kk[rows, :] * a[rows, :] * gi
        kh = k[rows, :] * gi
        al_s[rows, :] = (-kk[rows, :] * jnp.exp(cum - lwc)).astype(BF16)
        rt_s[rows, :] = (r[rows, :] * g).astype(BF16)
        be_s[rows, :] = be.astype(BF16)
        kh_s[rows, :] = kh.astype(BF16)
        bc_s[rows, :] = (be * g_end).astype(BF16)
        kc_s[rows, :] = (kh * g_end).astype(BF16)
        gend_s[8 * c:8 * c + 1, :] = g_end

    i2 = lax.broadcasted_iota(jnp.int32, (2 * CHUNK, 2 * CHUNK), 0)
    j2 = lax.broadcasted_iota(jnp.int32, (2 * CHUNK, 2 * CHUNK), 1)
    same = (i2 >> CHUNK_SHIFT) == (j2 >> CHUNK_SHIFT)
    strict = same & ((j2 & (CHUNK - 1)) < (i2 & (CHUNK - 1)))
    incl = same & ((j2 & (CHUNK - 1)) <= (i2 & (CHUNK - 1)))
    first = lax.broadcasted_iota(jnp.int32, (CHUNK, LANES), 1) < A_HEAD

    def chunk_body(c, carry):
        rows = pl.ds(pl.multiple_of(c * CHUNK, CHUNK), CHUNK)
        grow = pl.ds(pl.multiple_of(c * 8, 8), 1)
        pairs = [(al_s[rows, b], rt_s[rows, b], be_s[rows, b], kh_s[rows, b], v_s[rows, b], bc_s[rows, b],
                  kc_s[rows, b], gend_s[grow, b], state_ref[j]) for j, b in enumerate(blocks)]
        ys, s_new = _rwkv_chunks(pairs, strict, incl, first)
        for j, b in enumerate(blocks):
            y_s[rows, b] = ys[j]
            state_ref[j] = s_new[j]
        return carry

    lax.fori_loop(0, tb // CHUNK, chunk_body, 0)

    y = y_s[...]
    inv = 1.0 / A_HEAD
    mean = segsum(y) * inv
    yc = y - mean
    var = segsum(yc * yc) * inv
    y = yc * lax.rsqrt(var + RWKV_GN_EPS) * gng_ref[...] + gnb_ref[...]
    o_ref[...] = (y * gate_s[...] + bonus_s[...]).astype(o_ref.dtype)


def _rwkv(proj, prm, bsz, seq, tb):
    t = proj.shape[0]
    nt = seq // tb
    (mu, w0, a0, wch, wcl, guh, gul, kkp, kap, rk, gng, gnb) = prm
    seg = jnp.kron(jnp.eye(2, dtype=F32), jnp.ones((A_HEAD, A_HEAD), F32)).astype(BF16)
    tri = jnp.tril(jnp.ones((CHUNK, CHUNK), F32)).astype(BF16)
    full = lambda a_: pl.BlockSpec(a_.shape, lambda b, i: (0,) * a_.ndim)
    vm = lambda dt: pltpu.VMEM((tb, A_WIDTH), dt)
    return pl.pallas_call(
        functools.partial(_rwkv_kernel, tb=tb),
        grid=(bsz, nt),
        in_specs=[pl.BlockSpec((tb, A_IN), lambda b, i: (b * nt + i, 0))]
        + [full(x) for x in (mu, w0, a0, wch, wcl, guh, gul, kkp, kap, rk, gng, gnb, seg, tri)],
        out_specs=pl.BlockSpec((tb, A_WIDTH), lambda b, i: (b * nt + i, 0)),
        out_shape=jax.ShapeDtypeStruct((t, A_WIDTH), BF16),
        scratch_shapes=[pltpu.VMEM((A_PAIRS, LANES, LANES), F32), pltpu.VMEM((8, A_IN), F32)]
        + [vm(BF16) for _ in range(7)] + [pltpu.VMEM((tb // CHUNK * 8, A_WIDTH), F32)]
        + [vm(F32) for _ in range(3)],
        compiler_params=pltpu.CompilerParams(
            dimension_semantics=("arbitrary", "arbitrary"), vmem_limit_bytes=VMEM_LIMIT),
        name="rwkv7",
    )(proj, mu, w0, a0, wch, wcl, guh, gul, kkp, kap, rk, gng, gnb, seg, tri)


def _rwkv_params(mu, w0, w_up, a0, a_up, g_up, k_k, k_a, r_k, gn_g, gn_b):
    row = lambda x: x.reshape(1, -1).astype(F32)
    zeros = jnp.zeros_like(w_up)
    wc = jnp.concatenate([jnp.concatenate([w_up, zeros], 1), jnp.concatenate([zeros, a_up], 1)], 0)
    hl = lambda m: (m.astype(BF16), (m - m.astype(BF16).astype(F32)).astype(BF16))
    wch, wcl = hl(wc)
    guh, gul = hl(g_up)
    return (row(mu), row(w0), row(a0), wch, wcl, guh, gul, row(k_k), row(k_a), row(r_k), row(gn_g),
            row(gn_b))


def _pick(n, pref):
    t = min(n, pref)
    while n % t:
        t //= 2
    return t


def kernel(x, norm_mix_g, w_in, rwkv_mu, rwkv_w0, rwkv_w_up, rwkv_a0, rwkv_a_up, rwkv_g_up, rwkv_k_k, rwkv_k_a, rwkv_r_k, rwkv_gn_g, rwkv_gn_b, s5_a_re, s5_a_im, s5_log_dt, s5_b_re, s5_b_im, s5_c_re, s5_c_im, s5_d, s5_w_glu, s5_b_glu, s5_beta, diff_lq1, diff_lk1, diff_lq2, diff_lk2, diff_subln_g, w_out, norm_ff_g, w_ff1, w_ff2, norm_final_g):
    bsz, seq, d = x.shape
    t = bsz * seq
    depth = w_in.shape[0]
    x2 = x.reshape(t, d).astype(F32)
    tm = _pick(t, 512)
    tm_in = _pick(t, 1024)
    tq = _pick(seq, 1024)
    tb = _pick(seq, 256)
    assert tq & (tq - 1) == 0 and (seq // S5_LC) & (seq // S5_LC - 1) == 0 and tb % CHUNK == 0
    rows_per_batch = seq // S5_LC
    n_levels = max(1, (rows_per_batch - 1).bit_length())
    for l in range(depth):
        proj, proj_c = _inproj(x2, norm_mix_g[l], w_in[l].astype(BF16), tm_in, 768, A_IN + B_WIDTH)
        prm = _rwkv_params(rwkv_mu[l], rwkv_w0[l], rwkv_w_up[l], rwkv_a0[l], rwkv_a_up[l], rwkv_g_up[l],
                           rwkv_k_k[l], rwkv_k_a[l], rwkv_r_k[l].reshape(-1), rwkv_gn_g[l], rwkv_gn_b[l])
        y_a = _rwkv(proj, prm, bsz, seq, tb)
        s5p = _s5_params(s5_a_re[l], s5_a_im[l], s5_log_dt[l], s5_b_re[l], s5_b_im[l], s5_c_re[l],
                         s5_c_im[l], s5_d[l], s5_w_glu[l], s5_b_glu[l], s5_beta[l], n_levels)
        y_b = _s5(proj, s5p, bsz, seq)
        lambda_init = 0.8 - 0.6 * math.exp(-0.3 * l)
        lam = (jnp.exp(jnp.sum(diff_lq1[l] * diff_lk1[l])) - jnp.exp(jnp.sum(diff_lq2[l] * diff_lk2[l]))
               + lambda_init)
        y_c = _attn(proj_c, lam, diff_subln_g[l], bsz, seq, lambda_init, tq)
        x2 = _outproj(x2, y_a, y_b, y_c, w_out[l].astype(BF16), tm)
        x2 = _ffn(x2, norm_ff_g[l], w_ff1[l].astype(BF16), w_ff2[l].astype(BF16), norm_final_g,
                  l == depth - 1, tm, 1024)
    return x2.reshape(bsz, seq, d)
```

```python
import functools
import math

import jax
import jax.numpy as jnp
import numpy as np
from jax import lax
from jax.experimental import pallas as pl
from jax.experimental.pallas import tpu as pltpu

F32 = jnp.float32
BF16 = jnp.bfloat16

D_MODEL = 2048
DEPTH = 2
CHUNK = 64
CHUNK_SHIFT = 6
A_WIDTH = 768
A_HEAD = 64
A_PAIRS = A_WIDTH // (2 * A_HEAD)
A_DECAY_LORA = 64
A_ICLR_LORA = 64
A_GATE_LORA = 128
A_IN = 3 * A_WIDTH + A_DECAY_LORA + A_ICLR_LORA + A_GATE_LORA
B_WIDTH = 512
B_GROUP = 16
B_GROUPS = B_WIDTH // B_GROUP
B_STATE = 64
C_WIDTH = 768
C_HEAD = 64
C_VHEAD = 2 * C_HEAD
C_HEADS = C_WIDTH // C_VHEAD
C_IN = 3 * C_WIDTH
N_IN = A_IN + B_WIDTH + C_IN
D_FF = 4 * D_MODEL
RMS_EPS = 1e-6
SUBLN_EPS = 1e-5
RWKV_GN_EPS = 64e-5
NEG_INF = -1e30
LOG2E = 1.4426950408889634

LANES = 128
VMEM_LIMIT = 56 * 1024 * 1024
ATTN_CG = 256
ATTN_VROWS = C_VHEAD + 16
S5_LC = 64
S5_ROW = S5_LC * B_GROUP


def _dot(a, b):
    return jnp.dot(a, b, preferred_element_type=F32)


def _dot_nt(a, b):
    return lax.dot_general(a, b, (((1,), (1,)), ((), ())), preferred_element_type=F32)


def _split3(x):
    hi = x.astype(BF16)
    r1 = x - hi.astype(F32)
    mid = r1.astype(BF16)
    lo = (r1 - mid.astype(F32)).astype(BF16)
    return hi, mid, lo


def _dot_x3(x, m):
    hi, mid, lo = _split3(x)
    return _dot(hi, m) + _dot(mid, m) + _dot(lo, m)


def _dot_x2(x, m):
    hi = x.astype(BF16)
    mid = (x - hi.astype(F32)).astype(BF16)
    return _dot(hi, m) + _dot(mid, m)


def _dot_x3_left(m, x):
    hi, mid, lo = _split3(x)
    return _dot(m, hi) + _dot(m, mid) + _dot(m, lo)


def _dot_hl(x, m_hi, m_lo):
    hi, mid, _ = _split3(x)
    return _dot(hi, m_hi) + _dot(mid, m_hi) + _dot(hi, m_lo)


def _rms(x, g, eps):
    return x * lax.rsqrt(jnp.mean(x * x, axis=-1, keepdims=True) + eps) * g


def _inproj_kernel(x_ref, g_ref, w_ref, o32_ref, o16_ref, h_ref, *, n32):
    j = pl.program_id(1)

    @pl.when(j == 0)
    def _():
        h_ref[...] = _rms(x_ref[...], g_ref[...], RMS_EPS).astype(BF16)

    @pl.when(j < n32)
    def _():
        o32_ref[...] = _dot(h_ref[...], w_ref[...])

    @pl.when(j >= n32)
    def _():
        o16_ref[...] = _dot(h_ref[...], w_ref[...]).astype(BF16)


def _inproj(x2, g, w_bf, tm, tn, n_f32):
    t, d = x2.shape
    n = w_bf.shape[1]
    n32 = n_f32 // tn
    return pl.pallas_call(
        functools.partial(_inproj_kernel, n32=n32),
        grid=(t // tm, n // tn),
        in_specs=[pl.BlockSpec((tm, d), lambda i, j: (i, 0)),
                  pl.BlockSpec((1, d), lambda i, j: (0, 0)),
                  pl.BlockSpec((d, tn), lambda i, j: (0, j))],
        out_specs=[pl.BlockSpec((tm, tn), lambda i, j: (i, jnp.minimum(j, n32 - 1))),
                   pl.BlockSpec((tm, tn), lambda i, j: (i, jnp.maximum(j - n32, 0)))],
        out_shape=[jax.ShapeDtypeStruct((t, n_f32), F32), jax.ShapeDtypeStruct((t, n - n_f32), BF16)],
        scratch_shapes=[pltpu.VMEM((tm, d), BF16)],
        compiler_params=pltpu.CompilerParams(
            dimension_semantics=("arbitrary", "arbitrary"), vmem_limit_bytes=VMEM_LIMIT),
        name="inproj",
    )(x2, g.reshape(1, d), w_bf)


def _outproj_kernel(x_ref, ya_ref, yb_ref, yc_ref, w_ref, o_ref):
    b0, c0 = A_WIDTH, A_WIDTH + B_WIDTH
    acc = _dot(ya_ref[...].astype(BF16), w_ref[0:b0, :])
    acc += _dot(yb_ref[...].astype(BF16), w_ref[b0:c0, :])
    acc += _dot(yc_ref[...].astype(BF16), w_ref[c0:, :])
    o_ref[...] = x_ref[...] + acc


def _outproj(x2, ya, yb, yc, w_bf, tm):
    t, d = x2.shape
    row = lambda w: pl.BlockSpec((tm, w), lambda i: (i, 0))
    return pl.pallas_call(
        _outproj_kernel,
        grid=(t // tm,),
        in_specs=[row(d), row(A_WIDTH), row(B_WIDTH), row(C_WIDTH),
                  pl.BlockSpec(w_bf.shape, lambda i: (0, 0))],
        out_specs=row(d),
        out_shape=jax.ShapeDtypeStruct((t, d), F32),
        compiler_params=pltpu.CompilerParams(
            dimension_semantics=("arbitrary",), vmem_limit_bytes=VMEM_LIMIT),
        name="outproj",
    )(x2, ya, yb, yc, w_bf)


def _ffn_kernel(x_ref, g_ref, w1_ref, w2_ref, gf_ref, o_ref, h_ref, acc_ref, *, final_norm):
    f = pl.program_id(1)

    @pl.when(f == 0)
    def _():
        h_ref[...] = _rms(x_ref[...], g_ref[...], RMS_EPS).astype(BF16)
        acc_ref[...] = jnp.zeros_like(acc_ref)

    a = _dot(h_ref[...], w1_ref[...])
    a = jnp.square(jnp.maximum(a, 0.0))
    acc_ref[...] += _dot(a.astype(BF16), w2_ref[...])

    @pl.when(f == pl.num_programs(1) - 1)
    def _():
        y = x_ref[...] + acc_ref[...]
        if final_norm:
            y = _rms(y, gf_ref[...], RMS_EPS)
        o_ref[...] = y


def _ffn(x2, g, w1_bf, w2_bf, gf, final_norm, tm, tf):
    t, d = x2.shape
    dff = w1_bf.shape[1]
    return pl.pallas_call(
        functools.partial(_ffn_kernel, final_norm=final_norm),
        grid=(t // tm, dff // tf),
        in_specs=[pl.BlockSpec((tm, d), lambda i, j: (i, 0)),
                  pl.BlockSpec((1, d), lambda i, j: (0, 0)),
                  pl.BlockSpec((d, tf), lambda i, j: (0, j)),
                  pl.BlockSpec((tf, d), lambda i, j: (j, 0)),
                  pl.BlockSpec((1, d), lambda i, j: (0, 0))],
        out_specs=pl.BlockSpec((tm, d), lambda i, j: (i, 0)),
        out_shape=jax.ShapeDtypeStruct((t, d), F32),
        scratch_shapes=[pltpu.VMEM((tm, d), BF16), pltpu.VMEM((tm, d), F32)],
        compiler_params=pltpu.CompilerParams(
            dimension_semantics=("arbitrary", "arbitrary"), vmem_limit_bytes=VMEM_LIMIT),
        name="ffn",
    )(x2, g.reshape(1, d), w1_bf, w2_bf, gf.reshape(1, d))


def _attn_kernel(q_ref, k_ref, vt_ref, lam_ref, g_ref, o_ref,
                 qst_ref, st0_ref, st1_ref, pt0_ref, pt1_ref, mx0_ref, mx1_ref, al0_ref, al1_ref,
                 m_ref, acc_ref, *, tq, out_scale):
    qi = pl.program_id(2)
    st, pt, mx, al = (st0_ref, st1_ref), (pt0_ref, pt1_ref), (mx0_ref, mx1_ref), (al0_ref, al1_ref)
    qt = q_ref[...].astype(F32).T * (C_HEAD ** -0.5 * LOG2E)
    first = lax.broadcasted_iota(jnp.int32, qt.shape, 0) < C_HEAD
    qst_ref[:, 0:tq] = jnp.where(first, qt, 0.0).astype(BF16)
    qst_ref[:, tq:] = jnp.where(first, 0.0, qt).astype(BF16)
    m_ref[...] = jnp.full_like(m_ref, NEG_INF)
    acc_ref[...] = jnp.zeros_like(acc_ref)
    pt1_ref[...] = jnp.zeros_like(pt1_ref)
    al1_ref[...] = jnp.ones_like(al1_ref)
    groups = [slice(g * ATTN_CG, (g + 1) * ATTN_CG) for g in range(2 * tq // ATTN_CG)]

    def scores(tile, slot, masked, cg):
        k = k_ref[pl.ds(pl.multiple_of(tile * tq, tq), tq), :]
        s = _dot(k, qst_ref[:, cg])
        if masked:
            kpos = lax.broadcasted_iota(jnp.int32, s.shape, 0)
            qpos = (lax.broadcasted_iota(jnp.int32, s.shape, 1) + cg.start) & (tq - 1)
            s = jnp.where((kpos >> CHUNK_SHIFT) <= (qpos >> CHUNK_SHIFT), s, NEG_INF)
        st[slot][:, cg] = s
        mx[slot][:, cg] = jnp.max(s, axis=0, keepdims=True)

    def softmax(slot, cg):
        m_prev = m_ref[:, cg]
        m_new = jnp.maximum(m_prev, mx[slot][:, cg])
        m_ref[:, cg] = m_new
        al[slot][:, cg] = jnp.exp2(m_prev - m_new)
        for c0 in range(cg.start, cg.stop, LANES):
            cols = slice(c0, c0 + LANES)
            m_col = m_new[:, c0 - cg.start:c0 - cg.start + LANES]
            for r0 in range(0, tq, LANES):
                p = jnp.exp2(st[slot][r0:r0 + LANES, cols] - m_col)
                pt[slot][r0:r0 + LANES, cols] = p.astype(BF16)

    ones_tile = (lax.broadcasted_iota(jnp.int32, (ATTN_VROWS - C_VHEAD, tq), 0) == 0).astype(BF16)

    def accumulate(tile, slot, cg):
        vt_ones = jnp.concatenate([vt_ref[tile], ones_tile], axis=0)
        acc_ref[:, cg] = acc_ref[:, cg] * al[slot][:, cg] + _dot(vt_ones, pt[slot][:, cg])

    def step(n, slot):
        prev_tile = jnp.where(n <= 1, qi, n - 2)
        for cg in groups:
            scores(n, 1 - slot, False, cg)
            softmax(slot, cg)
            accumulate(prev_tile, 1 - slot, cg)

    for cg in groups:
        scores(qi, 0, True, cg)

    def body(i, carry):
        step(2 * i, 0)
        step(2 * i + 1, 1)
        return carry

    lax.fori_loop(0, qi >> 1, body, 0)
    tile_before_last = jnp.where(qi <= 1, qi, qi - 2)
    tile_last = jnp.where(qi == 0, qi, qi - 1)

    @pl.when((qi & 1) == 0)
    def _():
        for cg in groups:
            softmax(0, cg)
            accumulate(tile_before_last, 1, cg)
            accumulate(tile_last, 0, cg)

    @pl.when((qi & 1) == 1)
    def _():
        step(qi - 1, 0)
        for cg in groups:
            softmax(1, cg)
            accumulate(tile_before_last, 0, cg)
            accumulate(tile_last, 1, cg)

    o = acc_ref[0:C_VHEAD, :] * (1.0 / acc_ref[C_VHEAD:C_VHEAD + 1, :])
    o = o[:, 0:tq] - lam_ref[0:1, 0:1] * o[:, tq:]
    ms = jnp.mean(o * o, axis=0, keepdims=True)
    o = o * lax.rsqrt(ms + SUBLN_EPS) * (g_ref[...] * out_scale)
    o_ref[...] = o.T.astype(o_ref.dtype)


def _attn(proj_c, lam, subln_g, bsz, seq, lambda_init, tq):
    t = proj_c.shape[0]
    nq = seq // tq
    koff = C_WIDTH // LANES
    v = proj_c[:, 2 * C_WIDTH:].reshape(bsz, nq, tq, C_HEADS, C_VHEAD)
    vt = v.transpose(0, 3, 1, 4, 2)
    return pl.pallas_call(
        functools.partial(_attn_kernel, tq=tq, out_scale=1.0 - lambda_init),
        grid=(bsz, C_HEADS, nq),
        in_specs=[pl.BlockSpec((tq, LANES), lambda b, h, i: (b * nq + i, h)),
                  pl.BlockSpec((seq, LANES), lambda b, h, i: (b, koff + h)),
                  pl.BlockSpec((None, None, nq, C_VHEAD, tq), lambda b, h, i: (b, h, 0, 0, 0)),
                  pl.BlockSpec((1, LANES), lambda b, h, i: (0, 0)),
                  pl.BlockSpec((None, C_VHEAD, 1), lambda b, h, i: (h, 0, 0))],
        out_specs=pl.BlockSpec((tq, LANES), lambda b, h, i: (b * nq + i, h)),
        out_shape=jax.ShapeDtypeStruct((t, C_WIDTH), BF16),
        scratch_shapes=[pltpu.VMEM((C_VHEAD, 2 * tq), BF16)]
        + [pltpu.VMEM((tq, 2 * tq), F32)] * 2 + [pltpu.VMEM((tq, 2 * tq), BF16)] * 2
        + [pltpu.VMEM((1, 2 * tq), F32)] * 5 + [pltpu.VMEM((ATTN_VROWS, 2 * tq), F32)],
        compiler_params=pltpu.CompilerParams(
            dimension_semantics=("arbitrary", "arbitrary", "arbitrary"), vmem_limit_bytes=VMEM_LIMIT),
        name="diffattn",
    )(proj_c, proj_c, vt, jnp.broadcast_to(lam.reshape(1, 1), (1, LANES)).astype(F32),
      subln_g.reshape(C_HEADS, C_VHEAD, 1))


def _s5_params(a_re, a_im, log_dt, b_re, b_im, c_re, c_im, d_skip, w_glu, b_glu, beta, n_levels):
    hp = lax.Precision.HIGHEST
    g, p = a_re.shape
    dt = jnp.exp(log_dt)[:, None]
    mag = jnp.exp(dt * a_re)
    abr, abi = mag * jnp.cos(dt * a_im), mag * jnp.sin(dt * a_im)
    den = a_re * a_re + a_im * a_im
    zr = ((abr - 1.0) * a_re + abi * a_im) / den
    zi = (abi * a_re - (abr - 1.0) * a_im) / den
    bbr = zr[..., None] * b_re - zi[..., None] * b_im
    bbi = zr[..., None] * b_im + zi[..., None] * b_re
    pwr, pwi = jnp.ones((g, 1, p), F32), jnp.zeros((g, 1, p), F32)
    qr, qi = abr[:, None, :], abi[:, None, :]
    n = 1
    while n < S5_LC:
        pwr, pwi = (jnp.concatenate([pwr, pwr * qr - pwi * qi], 1),
                    jnp.concatenate([pwi, pwr * qi + pwi * qr], 1))
        qr, qi = qr * qr - qi * qi, 2.0 * qr * qi
        n *= 2
    lev1, lev2 = [], []
    for _ in range(n_levels):
        lev1.append(jnp.concatenate([qr, qr], -1))
        lev2.append(jnp.concatenate([-qi, qi], -1))
        qr, qi = qr * qr - qi * qi, 2.0 * qr * qi
    lev1 = jnp.concatenate(lev1, 1)
    lev2 = jnp.concatenate(lev2, 1)
    pw1r = jnp.concatenate([pwr[:, 1:], lev1[:, :1, :p]], 1)
    pw1i = jnp.concatenate([pwi[:, 1:], lev2[:, :1, p:]], 1)
    cpr = c_re[:, None] * pwr[:, :, None, :] - c_im[:, None] * pwi[:, :, None, :]
    cpi = c_re[:, None] * pwi[:, :, None, :] + c_im[:, None] * pwr[:, :, None, :]
    kj = (jnp.einsum('gjhp,gpk->gjhk', cpr, bbr, precision=hp)
          - jnp.einsum('gjhp,gpk->gjhk', cpi, bbi, precision=hp))
    kj = kj.at[:, 0].add(d_skip[:, :, None] * jnp.eye(B_GROUP, dtype=F32))
    lag = jnp.arange(S5_LC)[None, :] - jnp.arange(S5_LC)[:, None]
    kj_pad = jnp.concatenate([kj.transpose(0, 3, 1, 2).astype(BF16),
                              jnp.zeros((g, B_GROUP, 1, B_GROUP), BF16)], axis=2)
    kmat = kj_pad[:, :, jnp.where(lag >= 0, lag, S5_LC), :].reshape(g, S5_ROW, S5_ROW)
    rr, ri = pwr[:, ::-1], pwi[:, ::-1]
    bpr = rr[:, :, :, None] * bbr[:, None] - ri[:, :, :, None] * bbi[:, None]
    bpi = rr[:, :, :, None] * bbi[:, None] + ri[:, :, :, None] * bbr[:, None]
    bmat = jnp.concatenate([bpr.transpose(0, 3, 1, 2), bpi.transpose(0, 3, 1, 2)], -1)
    bmat = bmat.reshape(g, S5_ROW, 2 * p)
    cqr = c_re[:, None] * pw1r[:, :, None, :] - c_im[:, None] * pw1i[:, :, None, :]
    cqi = c_re[:, None] * pw1i[:, :, None, :] + c_im[:, None] * pw1r[:, :, None, :]
    cmat = jnp.concatenate([cqr.transpose(0, 3, 1, 2), -cqi.transpose(0, 3, 1, 2)], 1)
    cmat = cmat.reshape(g, 2 * p, S5_ROW)
    rep = LANES // B_GROUP
    eye = jnp.eye(rep, dtype=F32)
    glu = jnp.einsum('ab,ghk->gahbk', eye, w_glu).reshape(g, LANES, LANES)
    bglu = jnp.tile(b_glu, (1, rep)).reshape(g, 1, LANES)
    betat = jnp.tile(beta.reshape(g, B_GROUP), (1, S5_LC)).reshape(g, 1, S5_ROW)
    return (kmat.astype(BF16), bmat.astype(BF16), cmat.astype(BF16), glu.astype(BF16),
            bglu, betat, lev1, lev2)


def _gelu_tanh(x):
    return 0.5 * x * (1.0 + jnp.tanh(math.sqrt(2.0 / math.pi) * (x + 0.044715 * (x * x * x))))


def _s5_kernel(u_ref, kmat_ref, bmat_ref, cmat_ref, glu_ref, bglu_ref, beta_ref, lev1_ref, lev2_ref,
               mean_ref, o_ref, *, rows_per_batch, n_levels):
    u = u_ref[...]
    y = _dot(u, kmat_ref[...])
    x = _dot(u, bmat_ref[...])
    rpos = lax.broadcasted_iota(jnp.int32, x.shape, 0) & (rows_per_batch - 1)
    for lv in range(n_levels):
        d = 1 << lv
        if d >= rows_per_batch:
            break
        sh = jnp.where(rpos >= d, pltpu.roll(x, d, axis=0), 0.0)
        sw = pltpu.roll(sh, B_STATE, axis=1)
        x = x + lev1_ref[lv:lv + 1, :] * sh + lev2_ref[lv:lv + 1, :] * sw
    xs = jnp.where(rpos >= 1, pltpu.roll(x, 1, axis=0), 0.0)
    y = y + _dot(xs.astype(BF16), cmat_ref[...])
    y = _gelu_tanh(y)
    glu = glu_ref[...]
    mean = mean_ref[...]
    for c in range(S5_ROW // LANES):
        sl = slice(c * LANES, (c + 1) * LANES)
        yc = y[:, sl]
        z = _dot(yc.astype(BF16), glu) + bglu_ref[...]
        yc = yc * jax.nn.sigmoid(z)
        ms = _dot_x3(yc * yc, mean)
        o_ref[:, sl] = (yc * lax.rsqrt(ms + RMS_EPS) * beta_ref[:, sl]).astype(o_ref.dtype)


def _s5(proj, params, bsz, seq):
    t = proj.shape[0]
    rows = t // S5_LC
    rows_per_batch = seq // S5_LC
    n_levels = max(1, (rows_per_batch - 1).bit_length())
    kmat, bmat, cmat, glu, bglu, betat, lev1, lev2 = params
    u = proj[:, A_IN:A_IN + B_WIDTH].astype(BF16).reshape(rows, S5_LC, B_GROUPS, B_GROUP)
    u = u.transpose(2, 0, 3, 1).reshape(B_GROUPS, rows, S5_ROW)
    rep = LANES // B_GROUP
    mean = jnp.kron(jnp.eye(rep, dtype=F32), jnp.full((B_GROUP, B_GROUP), 1.0 / B_GROUP, F32)).astype(BF16)
    per_g = lambda a: pl.BlockSpec((None,) + a.shape[1:], lambda g: (g,) + (0,) * (a.ndim - 1))
    y = pl.pallas_call(
        functools.partial(_s5_kernel, rows_per_batch=rows_per_batch, n_levels=n_levels),
        grid=(B_GROUPS,),
        in_specs=[per_g(u), per_g(kmat), per_g(bmat), per_g(cmat), per_g(glu), per_g(bglu), per_g(betat),
                  per_g(lev1), per_g(lev2), pl.BlockSpec(mean.shape, lambda g: (0, 0))],
        out_specs=pl.BlockSpec((None, rows, S5_ROW), lambda g: (g, 0, 0)),
        out_shape=jax.ShapeDtypeStruct((B_GROUPS, rows, S5_ROW), BF16),
        compiler_params=pltpu.CompilerParams(
            dimension_semantics=("arbitrary",), vmem_limit_bytes=VMEM_LIMIT),
        name="s5",
    )(u, kmat, bmat, cmat, glu, bglu, betat, lev1, lev2, mean)
    return y.reshape(B_GROUPS, rows, S5_LC, B_GROUP).transpose(1, 2, 0, 3).reshape(t, B_WIDTH)


def _rwkv_chunks(pairs, strict, incl, first):
    c = CHUNK
    zero = jnp.zeros((), BF16)
    st = lambda x: jnp.concatenate([jnp.where(first, x, zero), jnp.where(first, zero, x)], axis=0)
    cat = lambda xs: jnp.concatenate(xs, axis=0)
    v_st = [st(p[4]) for p in pairs]
    gmat = [_dot_nt(cat([p[0], p[0], p[1], p[1]]), cat([st(p[2]), st(p[3])])) for p in pairs]
    ls = [_dot_nt(cat([st(p[0]), st(p[1])]), p[8].astype(BF16)) for p in pairs]
    a_ab = [jnp.where(strict, g[:2 * c, :2 * c], 0.0).astype(BF16) for g in gmat]
    a_k = [cat([jnp.where(strict, g[:2 * c, 2 * c:], 0.0), jnp.where(incl, g[2 * c:, 2 * c:], 0.0)]).astype(BF16)
           for g in gmat]
    a_rb = [jnp.where(incl, g[2 * c:, :2 * c], 0.0).astype(BF16) for g in gmat]
    av = [_dot(ak, vs) for ak, vs in zip(a_k, v_st)]
    u = [l[:2 * c] + a[:2 * c] for l, a in zip(ls, av)]
    pw = a_ab
    n = 1
    while n < c:
        if 2 * n >= c:
            x = [_dot(p_, u_.astype(BF16)) for p_, u_ in zip(pw, u)]
            u = [u_ + x_ for u_, x_ in zip(u, x)]
        else:
            x = [_dot(p_, jnp.concatenate([u_.astype(BF16), p_], axis=1)) for p_, u_ in zip(pw, u)]
            u = [u_ + x_[:, :2 * c] for u_, x_ in zip(u, x)]
            pw = [x_[:, 2 * c:].astype(BF16) for x_ in x]
        n *= 2
    yst = [l[2 * c:] + a[2 * c:] + _dot(arb, u_.astype(BF16)) for l, a, arb, u_ in zip(ls, av, a_rb, u)]
    ys = [y_[:c] + y_[c:] for y_ in yst]
    s_new = []
    for p, u_, vs in zip(pairs, u, v_st):
        lhs_t = cat([u_, vs.astype(F32)]).T.astype(BF16)
        s_new.append(p[8] * p[7] + _dot(lhs_t, cat([st(p[5]), st(p[6])])))
    return ys, s_new


def _rwkv_kernel(p_ref, mu_ref, w0_ref, a0_ref, wch_ref, wcl_ref, guh_ref, gul_ref, kkp_ref, kap_ref,
                 rk_ref, gng_ref, gnb_ref, seg_ref, tri_ref, o_ref,
                 state_ref, carry_ref, al_s, rt_s, be_s, kh_s, v_s, bc_s, kc_s, gend_s, y_s, bonus_s,
                 gate_s, *, tb, nb):
    ti = pl.program_id(0)

    @pl.when(ti == 0)
    def _():
        state_ref[...] = jnp.zeros_like(state_ref)
        carry_ref[...] = jnp.zeros_like(carry_ref)

    p = p_ref[...].reshape(nb * tb, A_IN)
    row = lax.broadcasted_iota(jnp.int32, p.shape, 0)
    prev = pltpu.roll(p, 1, axis=0)
    for bi in range(nb):
        prev = jnp.where(row == bi * tb, carry_ref[bi:bi + 1, :], prev)
        carry_ref[bi:bi + 1, :] = p[(bi + 1) * tb - 1:(bi + 1) * tb, :]
    xs = p + (prev - p) * mu_ref[...]
    r = xs[:, 0:A_WIDTH]
    k = xs[:, A_WIDTH:2 * A_WIDTH]
    v = xs[:, 2 * A_WIDTH:3 * A_WIDTH]
    z = xs[:, 3 * A_WIDTH:3 * A_WIDTH + LANES]
    lane = lax.broadcasted_iota(jnp.int32, z.shape, 1)
    z = jnp.where(lane < A_DECAY_LORA, jnp.tanh(z), z)
    wa = _dot_hl(z, wch_ref[...], wcl_ref[...])
    w = -jax.nn.softplus(-(w0_ref[...] + wa[:, :A_WIDTH])) - 0.5
    lw = -jnp.exp(w)
    a = jax.nn.sigmoid(a0_ref[...] + wa[:, A_WIDTH:])
    gate = _dot_hl(jax.nn.sigmoid(xs[:, 3 * A_WIDTH + LANES:]), guh_ref[...], gul_ref[...])
    kk = k * kkp_ref[...]
    k = k * (1.0 + (a - 1.0) * kap_ref[...])
    seg = seg_ref[...]
    blocks = [slice(j * LANES, (j + 1) * LANES) for j in range(A_PAIRS)]
    segsum = lambda x: jnp.concatenate([_dot_x2(x[:, b], seg) for b in blocks], axis=1)
    kk = kk * lax.rsqrt(segsum(kk * kk) + 1e-12)
    gate_s[...] = gate
    bonus_s[...] = segsum(r * k * rk_ref[...]) * v * gate
    v_s[...] = v.astype(BF16)
    tri = tri_ref[...]
    for c in range(nb * tb // CHUNK):
        rows = slice(c * CHUNK, (c + 1) * CHUNK)
        lwc = lw[rows, :]
        cum = _dot_x3_left(tri, lwc)
        g = jnp.exp(cum)
        gi = jnp.exp(-cum)
        g_end = g[CHUNK - 1:CHUNK, :]
        be = kk[rows, :] * a[rows, :] * gi
        kh = k[rows, :] * gi
        al_s[rows, :] = (-kk[rows, :] * jnp.exp(cum - lwc)).astype(BF16)
        rt_s[rows, :] = (r[rows, :] * g).astype(BF16)
        be_s[rows, :] = be.astype(BF16)
        kh_s[rows, :] = kh.astype(BF16)
        bc_s[rows, :] = (be * g_end).astype(BF16)
        kc_s[rows, :] = (kh * g_end).astype(BF16)
        gend_s[8 * c:8 * c + 1, :] = g_end

    i2 = lax.broadcasted_iota(jnp.int32, (2 * CHUNK, 2 * CHUNK), 0)
    j2 = lax.broadcasted_iota(jnp.int32, (2 * CHUNK, 2 * CHUNK), 1)
    same = (i2 >> CHUNK_SHIFT) == (j2 >> CHUNK_SHIFT)
    strict = same & ((j2 & (CHUNK - 1)) < (i2 & (CHUNK - 1)))
    incl = same & ((j2 & (CHUNK - 1)) <= (i2 & (CHUNK - 1)))
    first = lax.broadcasted_iota(jnp.int32, (CHUNK, LANES), 1) < A_HEAD

    def chunk_body(c, carry):
        where = []
        for bi in range(nb):
            rows = pl.ds(pl.multiple_of(bi * tb + c * CHUNK, CHUNK), CHUNK)
            grow = pl.ds(pl.multiple_of((bi * (tb // CHUNK) + c) * 8, 8), 1)
            where += [(rows, grow, b, bi * A_PAIRS + j) for j, b in enumerate(blocks)]
        pairs = [(al_s[rows, b], rt_s[rows, b], be_s[rows, b], kh_s[rows, b], v_s[rows, b], bc_s[rows, b],
                  kc_s[rows, b], gend_s[grow, b], state_ref[sj]) for rows, grow, b, sj in where]
        ys, s_new = _rwkv_chunks(pairs, strict, incl, first)
        for (rows, _, b, sj), y_, s_ in zip(where, ys, s_new):
            y_s[rows, b] = y_
            state_ref[sj] = s_
        return carry

    lax.fori_loop(0, tb // CHUNK, chunk_body, 0)

    y = y_s[...]
    inv = 1.0 / A_HEAD
    mean = segsum(y) * inv
    yc = y - mean
    var = segsum(yc * yc) * inv
    y = yc * lax.rsqrt(var + RWKV_GN_EPS) * gng_ref[...] + gnb_ref[...]
    o_ref[...] = (y * gate_s[...] + bonus_s[...]).astype(o_ref.dtype).reshape(nb, tb, A_WIDTH)


def _rwkv(proj, prm, bsz, seq, tb):
    t = proj.shape[0]
    nt = seq // tb
    (mu, w0, a0, wch, wcl, guh, gul, kkp, kap, rk, gng, gnb) = prm
    seg = jnp.kron(jnp.eye(2, dtype=F32), jnp.ones((A_HEAD, A_HEAD), F32)).astype(BF16)
    tri = jnp.tril(jnp.ones((CHUNK, CHUNK), F32)).astype(BF16)
    full = lambda a_: pl.BlockSpec(a_.shape, lambda i: (0,) * a_.ndim)
    vm = lambda dt: pltpu.VMEM((bsz * tb, A_WIDTH), dt)
    assert bsz <= 8
    y = pl.pallas_call(
        functools.partial(_rwkv_kernel, tb=tb, nb=bsz),
        grid=(nt,),
        in_specs=[pl.BlockSpec((bsz, tb, A_IN), lambda i: (0, i, 0))]
        + [full(x) for x in (mu, w0, a0, wch, wcl, guh, gul, kkp, kap, rk, gng, gnb, seg, tri)],
        out_specs=pl.BlockSpec((bsz, tb, A_WIDTH), lambda i: (0, i, 0)),
        out_shape=jax.ShapeDtypeStruct((bsz, seq, A_WIDTH), BF16),
        scratch_shapes=[pltpu.VMEM((bsz * A_PAIRS, LANES, LANES), F32), pltpu.VMEM((8, A_IN), F32)]
        + [vm(BF16) for _ in range(7)] + [pltpu.VMEM((bsz * tb // CHUNK * 8, A_WIDTH), F32)]
        + [vm(F32) for _ in range(3)],
        compiler_params=pltpu.CompilerParams(
            dimension_semantics=("arbitrary",), vmem_limit_bytes=VMEM_LIMIT),
        name="rwkv7",
    )(proj.reshape(bsz, seq, proj.shape[1]), mu, w0, a0, wch, wcl, guh, gul, kkp, kap, rk, gng, gnb, seg, tri)
    return y.reshape(t, A_WIDTH)


def _rwkv_params(mu, w0, w_up, a0, a_up, g_up, k_k, k_a, r_k, gn_g, gn_b):
    row = lambda x: x.reshape(1, -1).astype(F32)
    zeros = jnp.zeros_like(w_up)
    wc = jnp.concatenate([jnp.concatenate([w_up, zeros], 1), jnp.concatenate([zeros, a_up], 1)], 0)
    hl = lambda m: (m.astype(BF16), (m - m.astype(BF16).astype(F32)).astype(BF16))
    wch, wcl = hl(wc)
    guh, gul = hl(g_up)
    return (row(mu), row(w0), row(a0), wch, wcl, guh, gul, row(k_k), row(k_a), row(r_k), row(gn_g),
            row(gn_b))


def _pick(n, pref):
    t = min(n, pref)
    while n % t:
        t //= 2
    return t


def kernel(x, norm_mix_g, w_in, rwkv_mu, rwkv_w0, rwkv_w_up, rwkv_a0, rwkv_a_up, rwkv_g_up, rwkv_k_k, rwkv_k_a, rwkv_r_k, rwkv_gn_g, rwkv_gn_b, s5_a_re, s5_a_im, s5_log_dt, s5_b_re, s5_b_im, s5_c_re, s5_c_im, s5_d, s5_w_glu, s5_b_glu, s5_beta, diff_lq1, diff_lk1, diff_lq2, diff_lk2, diff_subln_g, w_out, norm_ff_g, w_ff1, w_ff2, norm_final_g):
    bsz, seq, d = x.shape
    t = bsz * seq
    depth = w_in.shape[0]
    x2 = x.reshape(t, d).astype(F32)
    tm = _pick(t, 512)
    tm_in = _pick(t, 1024)
    tq = _pick(seq, 1024)
    tb = _pick(seq, 256)
    assert tq & (tq - 1) == 0 and (seq // S5_LC) & (seq // S5_LC - 1) == 0 and tb % CHUNK == 0
    rows_per_batch = seq // S5_LC
    n_levels = max(1, (rows_per_batch - 1).bit_length())
    for l in range(depth):
        proj, proj_c = _inproj(x2, norm_mix_g[l], w_in[l].astype(BF16), tm_in, 768, A_IN + B_WIDTH)
        prm = _rwkv_params(rwkv_mu[l], rwkv_w0[l], rwkv_w_up[l], rwkv_a0[l], rwkv_a_up[l], rwkv_g_up[l],
                           rwkv_k_k[l], rwkv_k_a[l], rwkv_r_k[l].reshape(-1), rwkv_gn_g[l], rwkv_gn_b[l])
        y_a = _rwkv(proj, prm, bsz, seq, tb)
        s5p = _s5_params(s5_a_re[l], s5_a_im[l], s5_log_dt[l], s5_b_re[l], s5_b_im[l], s5_c_re[l],
                         s5_c_im[l], s5_d[l], s5_w_glu[l], s5_b_glu[l], s5_beta[l], n_levels)
        y_b = _s5(proj, s5p, bsz, seq)
        lambda_init = 0.8 - 0.6 * math.exp(-0.3 * l)
        lam = (jnp.exp(jnp.sum(diff_lq1[l] * diff_lk1[l])) - jnp.exp(jnp.sum(diff_lq2[l] * diff_lk2[l]))
               + lambda_init)
        y_c = _attn(proj_c, lam, diff_subln_g[l], bsz, seq, lambda_init, tq)
        x2 = _outproj(x2, y_a, y_b, y_c, w_out[l].astype(BF16), tm)
        x2 = _ffn(x2, norm_ff_g[l], w_ff1[l].astype(BF16), w_ff2[l].astype(BF16), norm_final_g,
                  l == depth - 1, tm, 1024)
    return x2.reshape(bsz, seq, d)
```

```python
import functools
import math

import jax
import jax.numpy as jnp
import numpy as np
from jax import lax
from jax.experimental import pallas as pl
from jax.experimental.pallas import tpu as pltpu

F32 = jnp.float32
BF16 = jnp.bfloat16

D_MODEL = 2048
DEPTH = 2
CHUNK = 64
CHUNK_SHIFT = 6
A_WIDTH = 768
A_HEAD = 64
A_PAIRS = A_WIDTH // (2 * A_HEAD)
A_DECAY_LORA = 64
A_ICLR_LORA = 64
A_GATE_LORA = 128
A_IN = 3 * A_WIDTH + A_DECAY_LORA + A_ICLR_LORA + A_GATE_LORA
B_WIDTH = 512
B_GROUP = 16
B_GROUPS = B_WIDTH // B_GROUP
B_STATE = 64
C_WIDTH = 768
C_HEAD = 64
C_VHEAD = 2 * C_HEAD
C_HEADS = C_WIDTH // C_VHEAD
C_IN = 3 * C_WIDTH
N_IN = A_IN + B_WIDTH + C_IN
D_FF = 4 * D_MODEL
RMS_EPS = 1e-6
SUBLN_EPS = 1e-5
RWKV_GN_EPS = 64e-5
NEG_INF = -1e30
LOG2E = 1.4426950408889634

LANES = 128
VMEM_LIMIT = 56 * 1024 * 1024
ATTN_CG = 256
ATTN_VROWS = C_VHEAD + 16
S5_LC = 64
S5_ROW = S5_LC * B_GROUP


def _dot(a, b):
    return jnp.dot(a, b, preferred_element_type=F32)


def _dot_nt(a, b):
    return lax.dot_general(a, b, (((1,), (1,)), ((), ())), preferred_element_type=F32)


def _split3(x):
    hi = x.astype(BF16)
    r1 = x - hi.astype(F32)
    mid = r1.astype(BF16)
    lo = (r1 - mid.astype(F32)).astype(BF16)
    return hi, mid, lo


def _dot_x3(x, m):
    hi, mid, lo = _split3(x)
    return _dot(hi, m) + _dot(mid, m) + _dot(lo, m)


def _dot_x2(x, m):
    hi = x.astype(BF16)
    mid = (x - hi.astype(F32)).astype(BF16)
    return _dot(hi, m) + _dot(mid, m)


def _dot_x3_left(m, x):
    hi, mid, lo = _split3(x)
    return _dot(m, hi) + _dot(m, mid) + _dot(m, lo)


def _dot_hl(x, m_hi, m_lo):
    hi, mid, _ = _split3(x)
    return _dot(hi, m_hi) + _dot(mid, m_hi) + _dot(hi, m_lo)


def _rms(x, g, eps):
    return x * lax.rsqrt(jnp.mean(x * x, axis=-1, keepdims=True) + eps) * g


def _inproj_kernel(x_ref, g_ref, w_ref, o32_ref, o16_ref, h_ref, *, n32):
    j = pl.program_id(1)

    @pl.when(j == 0)
    def _():
        h_ref[...] = _rms(x_ref[...], g_ref[...], RMS_EPS).astype(BF16)

    @pl.when(j < n32)
    def _():
        o32_ref[...] = _dot(h_ref[...], w_ref[...])

    @pl.when(j >= n32)
    def _():
        o16_ref[...] = _dot(h_ref[...], w_ref[...]).astype(BF16)


def _inproj(x2, g, w_bf, tm, tn, n_f32):
    t, d = x2.shape
    n = w_bf.shape[1]
    n32 = n_f32 // tn
    return pl.pallas_call(
        functools.partial(_inproj_kernel, n32=n32),
        grid=(t // tm, n // tn),
        in_specs=[pl.BlockSpec((tm, d), lambda i, j: (i, 0)),
                  pl.BlockSpec((1, d), lambda i, j: (0, 0)),
                  pl.BlockSpec((d, tn), lambda i, j: (0, j))],
        out_specs=[pl.BlockSpec((tm, tn), lambda i, j: (i, jnp.minimum(j, n32 - 1))),
                   pl.BlockSpec((tm, tn), lambda i, j: (i, jnp.maximum(j - n32, 0)))],
        out_shape=[jax.ShapeDtypeStruct((t, n_f32), F32), jax.ShapeDtypeStruct((t, n - n_f32), BF16)],
        scratch_shapes=[pltpu.VMEM((tm, d), BF16)],
        compiler_params=pltpu.CompilerParams(
            dimension_semantics=("arbitrary", "arbitrary"), vmem_limit_bytes=VMEM_LIMIT),
        name="inproj",
    )(x2, g.reshape(1, d), w_bf)


def _outproj_kernel(x_ref, ya_ref, yb_ref, yc_ref, w_ref, o_ref):
    b0, c0 = A_WIDTH, A_WIDTH + B_WIDTH
    acc = _dot(ya_ref[...].astype(BF16), w_ref[0:b0, :])
    acc += _dot(yb_ref[...].astype(BF16), w_ref[b0:c0, :])
    acc += _dot(yc_ref[...].astype(BF16), w_ref[c0:, :])
    o_ref[...] = x_ref[...] + acc


def _outproj(x2, ya, yb, yc, w_bf, tm):
    t, d = x2.shape
    row = lambda w: pl.BlockSpec((tm, w), lambda i: (i, 0))
    return pl.pallas_call(
        _outproj_kernel,
        grid=(t // tm,),
        in_specs=[row(d), row(A_WIDTH), row(B_WIDTH), row(C_WIDTH),
                  pl.BlockSpec(w_bf.shape, lambda i: (0, 0))],
        out_specs=row(d),
        out_shape=jax.ShapeDtypeStruct((t, d), F32),
        compiler_params=pltpu.CompilerParams(
            dimension_semantics=("arbitrary",), vmem_limit_bytes=VMEM_LIMIT),
        name="outproj",
    )(x2, ya, yb, yc, w_bf)


def _ffn_kernel(x_ref, g_ref, w1_ref, w2_ref, gf_ref, o_ref, h_ref, acc_ref, *, final_norm):
    f = pl.program_id(1)

    @pl.when(f == 0)
    def _():
        h_ref[...] = _rms(x_ref[...], g_ref[...], RMS_EPS).astype(BF16)
        acc_ref[...] = jnp.zeros_like(acc_ref)

    a = _dot(h_ref[...], w1_ref[...])
    a = jnp.square(jnp.maximum(a, 0.0))
    acc_ref[...] += _dot(a.astype(BF16), w2_ref[...])

    @pl.when(f == pl.num_programs(1) - 1)
    def _():
        y = x_ref[...] + acc_ref[...]
        if final_norm:
            y = _rms(y, gf_ref[...], RMS_EPS)
        o_ref[...] = y


def _ffn(x2, g, w1_bf, w2_bf, gf, final_norm, tm, tf):
    t, d = x2.shape
    dff = w1_bf.shape[1]
    return pl.pallas_call(
        functools.partial(_ffn_kernel, final_norm=final_norm),
        grid=(t // tm, dff // tf),
        in_specs=[pl.BlockSpec((tm, d), lambda i, j: (i, 0)),
                  pl.BlockSpec((1, d), lambda i, j: (0, 0)),
                  pl.BlockSpec((d, tf), lambda i, j: (0, j)),
                  pl.BlockSpec((tf, d), lambda i, j: (j, 0)),
                  pl.BlockSpec((1, d), lambda i, j: (0, 0))],
        out_specs=pl.BlockSpec((tm, d), lambda i, j: (i, 0)),
        out_shape=jax.ShapeDtypeStruct((t, d), F32),
        scratch_shapes=[pltpu.VMEM((tm, d), BF16), pltpu.VMEM((tm, d), F32)],
        compiler_params=pltpu.CompilerParams(
            dimension_semantics=("arbitrary", "arbitrary"), vmem_limit_bytes=VMEM_LIMIT),
        name="ffn",
    )(x2, g.reshape(1, d), w1_bf, w2_bf, gf.reshape(1, d))


def _attn_kernel(q_ref, k_ref, v_ref, lam_ref, g_ref, o_ref,
                 qst_ref, st0_ref, st1_ref, pt0_ref, pt1_ref, mx0_ref, mx1_ref, al0_ref, al1_ref,
                 m_ref, acc_ref, *, tq, out_scale):
    qi = pl.program_id(2)
    st, pt, mx, al = (st0_ref, st1_ref), (pt0_ref, pt1_ref), (mx0_ref, mx1_ref), (al0_ref, al1_ref)
    qt = q_ref[...].astype(F32).T * (C_HEAD ** -0.5 * LOG2E)
    first = lax.broadcasted_iota(jnp.int32, qt.shape, 0) < C_HEAD
    qst_ref[:, 0:tq] = jnp.where(first, qt, 0.0).astype(BF16)
    qst_ref[:, tq:] = jnp.where(first, 0.0, qt).astype(BF16)
    m_ref[...] = jnp.full_like(m_ref, NEG_INF)
    acc_ref[...] = jnp.zeros_like(acc_ref)
    pt1_ref[...] = jnp.zeros_like(pt1_ref)
    al1_ref[...] = jnp.ones_like(al1_ref)
    groups = [slice(g * ATTN_CG, (g + 1) * ATTN_CG) for g in range(2 * tq // ATTN_CG)]

    def scores(tile, slot, masked, cg):
        k = k_ref[pl.ds(pl.multiple_of(tile * tq, tq), tq), :]
        s = _dot(k, qst_ref[:, cg])
        if masked:
            kpos = lax.broadcasted_iota(jnp.int32, s.shape, 0)
            qpos = (lax.broadcasted_iota(jnp.int32, s.shape, 1) + cg.start) & (tq - 1)
            s = jnp.where((kpos >> CHUNK_SHIFT) <= (qpos >> CHUNK_SHIFT), s, NEG_INF)
        st[slot][:, cg] = s
        mx[slot][:, cg] = jnp.max(s, axis=0, keepdims=True)

    def softmax(slot, cg):
        m_prev = m_ref[:, cg]
        m_new = jnp.maximum(m_prev, mx[slot][:, cg])
        m_ref[:, cg] = m_new
        al[slot][:, cg] = jnp.exp2(m_prev - m_new)
        for c0 in range(cg.start, cg.stop, LANES):
            cols = slice(c0, c0 + LANES)
            m_col = m_new[:, c0 - cg.start:c0 - cg.start + LANES]
            for r0 in range(0, tq, LANES):
                p = jnp.exp2(st[slot][r0:r0 + LANES, cols] - m_col)
                pt[slot][r0:r0 + LANES, cols] = p.astype(BF16)

    ones_tile = (lax.broadcasted_iota(jnp.int32, (ATTN_VROWS - C_VHEAD, tq), 0) == 0).astype(BF16)

    def values_t(tile):
        v = v_ref[pl.ds(pl.multiple_of(tile * tq, tq), tq), :]
        return jnp.concatenate([v.T, ones_tile], axis=0)

    def accumulate(vt_ones, slot, cg):
        acc_ref[:, cg] = acc_ref[:, cg] * al[slot][:, cg] + _dot(vt_ones, pt[slot][:, cg])

    def step(n, slot):
        vt_prev = values_t(jnp.where(n <= 1, qi, n - 2))
        for cg in groups:
            scores(n, 1 - slot, False, cg)
            softmax(slot, cg)
            accumulate(vt_prev, 1 - slot, cg)

    for cg in groups:
        scores(qi, 0, True, cg)

    def body(i, carry):
        step(2 * i, 0)
        step(2 * i + 1, 1)
        return carry

    lax.fori_loop(0, qi >> 1, body, 0)
    tile_before_last = jnp.where(qi <= 1, qi, qi - 2)
    tile_last = jnp.where(qi == 0, qi, qi - 1)

    @pl.when((qi & 1) == 0)
    def _():
        vt_a, vt_b = values_t(tile_before_last), values_t(tile_last)
        for cg in groups:
            softmax(0, cg)
            accumulate(vt_a, 1, cg)
            accumulate(vt_b, 0, cg)

    @pl.when((qi & 1) == 1)
    def _():
        step(qi - 1, 0)
        vt_a, vt_b = values_t(tile_before_last), values_t(tile_last)
        for cg in groups:
            softmax(1, cg)
            accumulate(vt_a, 0, cg)
            accumulate(vt_b, 1, cg)

    o = acc_ref[0:C_VHEAD, :] * (1.0 / acc_ref[C_VHEAD:C_VHEAD + 1, :])
    o = o[:, 0:tq] - lam_ref[0:1, 0:1] * o[:, tq:]
    ms = jnp.mean(o * o, axis=0, keepdims=True)
    o = o * lax.rsqrt(ms + SUBLN_EPS) * (g_ref[...] * out_scale)
    o_ref[...] = o.T.astype(o_ref.dtype)


def _attn(proj_c, lam, subln_g, bsz, seq, lambda_init, tq):
    t = proj_c.shape[0]
    nq = seq // tq
    koff = C_WIDTH // LANES
    return pl.pallas_call(
        functools.partial(_attn_kernel, tq=tq, out_scale=1.0 - lambda_init),
        grid=(bsz, C_HEADS, nq),
        in_specs=[pl.BlockSpec((tq, LANES), lambda b, h, i: (b * nq + i, h)),
                  pl.BlockSpec((seq, LANES), lambda b, h, i: (b, koff + h)),
                  pl.BlockSpec((seq, LANES), lambda b, h, i: (b, 2 * koff + h)),
                  pl.BlockSpec((1, LANES), lambda b, h, i: (0, 0)),
                  pl.BlockSpec((None, C_VHEAD, 1), lambda b, h, i: (h, 0, 0))],
        out_specs=pl.BlockSpec((tq, LANES), lambda b, h, i: (b * nq + i, h)),
        out_shape=jax.ShapeDtypeStruct((t, C_WIDTH), BF16),
        scratch_shapes=[pltpu.VMEM((C_VHEAD, 2 * tq), BF16)]
        + [pltpu.VMEM((tq, 2 * tq), F32)] * 2 + [pltpu.VMEM((tq, 2 * tq), BF16)] * 2
        + [pltpu.VMEM((1, 2 * tq), F32)] * 5 + [pltpu.VMEM((ATTN_VROWS, 2 * tq), F32)],
        compiler_params=pltpu.CompilerParams(
            dimension_semantics=("arbitrary", "arbitrary", "arbitrary"), vmem_limit_bytes=VMEM_LIMIT),
        name="diffattn",
    )(proj_c, proj_c, proj_c, jnp.broadcast_to(lam.reshape(1, 1), (1, LANES)).astype(F32),
      subln_g.reshape(C_HEADS, C_VHEAD, 1))


def _s5_params(a_re, a_im, log_dt, b_re, b_im, c_re, c_im, d_skip, w_glu, b_glu, beta, n_levels):
    hp = lax.Precision.HIGHEST
    g, p = a_re.shape
    dt = jnp.exp(log_dt)[:, None]
    mag = jnp.exp(dt * a_re)
    abr, abi = mag * jnp.cos(dt * a_im), mag * jnp.sin(dt * a_im)
    den = a_re * a_re + a_im * a_im
    zr = ((abr - 1.0) * a_re + abi * a_im) / den
    zi = (abi * a_re - (abr - 1.0) * a_im) / den
    bbr = zr[..., None] * b_re - zi[..., None] * b_im
    bbi = zr[..., None] * b_im + zi[..., None] * b_re
    pwr, pwi = jnp.ones((g, 1, p), F32), jnp.zeros((g, 1, p), F32)
    qr, qi = abr[:, None, :], abi[:, None, :]
    n = 1
    while n < S5_LC:
        pwr, pwi = (jnp.concatenate([pwr, pwr * qr - pwi * qi], 1),
                    jnp.concatenate([pwi, pwr * qi + pwi * qr], 1))
        qr, qi = qr * qr - qi * qi, 2.0 * qr * qi
        n *= 2
    lev1, lev2 = [], []
    for _ in range(n_levels):
        lev1.append(jnp.concatenate([qr, qr], -1))
        lev2.append(jnp.concatenate([-qi, qi], -1))
        qr, qi = qr * qr - qi * qi, 2.0 * qr * qi
    lev1 = jnp.concatenate(lev1, 1)
    lev2 = jnp.concatenate(lev2, 1)
    pw1r = jnp.concatenate([pwr[:, 1:], lev1[:, :1, :p]], 1)
    pw1i = jnp.concatenate([pwi[:, 1:], lev2[:, :1, p:]], 1)
    cpr = c_re[:, None] * pwr[:, :, None, :] - c_im[:, None] * pwi[:, :, None, :]
    cpi = c_re[:, None] * pwi[:, :, None, :] + c_im[:, None] * pwr[:, :, None, :]
    kj = (jnp.einsum('gjhp,gpk->gjhk', cpr, bbr, precision=hp)
          - jnp.einsum('gjhp,gpk->gjhk', cpi, bbi, precision=hp))
    kj = kj.at[:, 0].add(d_skip[:, :, None] * jnp.eye(B_GROUP, dtype=F32))
    lag = jnp.arange(S5_LC)[None, :] - jnp.arange(S5_LC)[:, None]
    kj_pad = jnp.concatenate([kj.transpose(0, 3, 1, 2).astype(BF16),
                              jnp.zeros((g, B_GROUP, 1, B_GROUP), BF16)], axis=2)
    kmat = kj_pad[:, :, jnp.where(lag >= 0, lag, S5_LC), :].reshape(g, S5_ROW, S5_ROW)
    rr, ri = pwr[:, ::-1], pwi[:, ::-1]
    bpr = rr[:, :, :, None] * bbr[:, None] - ri[:, :, :, None] * bbi[:, None]
    bpi = rr[:, :, :, None] * bbi[:, None] + ri[:, :, :, None] * bbr[:, None]
    bmat = jnp.concatenate([bpr.transpose(0, 3, 1, 2), bpi.transpose(0, 3, 1, 2)], -1)
    bmat = bmat.reshape(g, S5_ROW, 2 * p)
    cqr = c_re[:, None] * pw1r[:, :, None, :] - c_im[:, None] * pw1i[:, :, None, :]
    cqi = c_re[:, None] * pw1i[:, :, None, :] + c_im[:, None] * pw1r[:, :, None, :]
    cmat = jnp.concatenate([cqr.transpose(0, 3, 1, 2), -cqi.transpose(0, 3, 1, 2)], 1)
    cmat = cmat.reshape(g, 2 * p, S5_ROW)
    rep = LANES // B_GROUP
    eye = jnp.eye(rep, dtype=F32)
    glu = jnp.einsum('ab,ghk->gahbk', eye, w_glu).reshape(g, LANES, LANES)
    bglu = jnp.tile(b_glu, (1, rep)).reshape(g, 1, LANES)
    betat = jnp.tile(beta.reshape(g, B_GROUP), (1, S5_LC)).reshape(g, 1, S5_ROW)
    return (kmat.astype(BF16), bmat.astype(BF16), cmat.astype(BF16), glu.astype(BF16),
            bglu, betat, lev1, lev2)


def _gelu_tanh(x):
    return 0.5 * x * (1.0 + jnp.tanh(math.sqrt(2.0 / math.pi) * (x + 0.044715 * (x * x * x))))


def _s5_kernel(u_ref, kmat_ref, bmat_ref, cmat_ref, glu_ref, bglu_ref, beta_ref, lev1_ref, lev2_ref,
               mean_ref, o_ref, *, rows_per_batch, n_levels):
    u = u_ref[...]
    y = _dot(u, kmat_ref[...])
    x = _dot(u, bmat_ref[...])
    rpos = lax.broadcasted_iota(jnp.int32, x.shape, 0) & (rows_per_batch - 1)
    for lv in range(n_levels):
        d = 1 << lv
        if d >= rows_per_batch:
            break
        sh = jnp.where(rpos >= d, pltpu.roll(x, d, axis=0), 0.0)
        sw = pltpu.roll(sh, B_STATE, axis=1)
        x = x + lev1_ref[lv:lv + 1, :] * sh + lev2_ref[lv:lv + 1, :] * sw
    xs = jnp.where(rpos >= 1, pltpu.roll(x, 1, axis=0), 0.0)
    y = y + _dot(xs.astype(BF16), cmat_ref[...])
    y = _gelu_tanh(y)
    glu = glu_ref[...]
    mean = mean_ref[...]
    for c in range(S5_ROW // LANES):
        sl = slice(c * LANES, (c + 1) * LANES)
        yc = y[:, sl]
        z = _dot(yc.astype(BF16), glu) + bglu_ref[...]
        yc = yc * jax.nn.sigmoid(z)
        ms = _dot_x3(yc * yc, mean)
        o_ref[:, sl] = (yc * lax.rsqrt(ms + RMS_EPS) * beta_ref[:, sl]).astype(o_ref.dtype)


def _s5(proj, params, bsz, seq):
    t = proj.shape[0]
    rows = t // S5_LC
    rows_per_batch = seq // S5_LC
    n_levels = max(1, (rows_per_batch - 1).bit_length())
    kmat, bmat, cmat, glu, bglu, betat, lev1, lev2 = params
    u = proj[:, A_IN:A_IN + B_WIDTH].astype(BF16).reshape(rows, S5_LC, B_GROUPS, B_GROUP)
    u = u.transpose(2, 0, 3, 1).reshape(B_GROUPS, rows, S5_ROW)
    rep = LANES // B_GROUP
    mean = jnp.kron(jnp.eye(rep, dtype=F32), jnp.full((B_GROUP, B_GROUP), 1.0 / B_GROUP, F32)).astype(BF16)
    per_g = lambda a: pl.BlockSpec((None,) + a.shape[1:], lambda g: (g,) + (0,) * (a.ndim - 1))
    y = pl.pallas_call(
        functools.partial(_s5_kernel, rows_per_batch=rows_per_batch, n_levels=n_levels),
        grid=(B_GROUPS,),
        in_specs=[per_g(u), per_g(kmat), per_g(bmat), per_g(cmat), per_g(glu), per_g(bglu), per_g(betat),
                  per_g(lev1), per_g(lev2), pl.BlockSpec(mean.shape, lambda g: (0, 0))],
        out_specs=pl.BlockSpec((None, rows, S5_ROW), lambda g: (g, 0, 0)),
        out_shape=jax.ShapeDtypeStruct((B_GROUPS, rows, S5_ROW), BF16),
        compiler_params=pltpu.CompilerParams(
            dimension_semantics=("arbitrary",), vmem_limit_bytes=VMEM_LIMIT),
        name="s5",
    )(u, kmat, bmat, cmat, glu, bglu, betat, lev1, lev2, mean)
    return y.reshape(B_GROUPS, rows, S5_LC, B_GROUP).transpose(1, 2, 0, 3).reshape(t, B_WIDTH)


def _rwkv_chunks(pairs, strict, incl, first):
    c = CHUNK
    zero = jnp.zeros((), BF16)
    st = lambda x: jnp.concatenate([jnp.where(first, x, zero), jnp.where(first, zero, x)], axis=0)
    cat = lambda xs: jnp.concatenate(xs, axis=0)
    v_st = [st(p[4]) for p in pairs]
    gmat = [_dot_nt(cat([p[0], p[0], p[1], p[1]]), cat([st(p[2]), st(p[3])])) for p in pairs]
    ls = [_dot_nt(cat([st(p[0]), st(p[1])]), p[8].astype(BF16)) for p in pairs]
    a_ab = [jnp.where(strict, g[:2 * c, :2 * c], 0.0).astype(BF16) for g in gmat]
    a_k = [cat([jnp.where(strict, g[:2 * c, 2 * c:], 0.0), jnp.where(incl, g[2 * c:, 2 * c:], 0.0)]).astype(BF16)
           for g in gmat]
    a_rb = [jnp.where(incl, g[2 * c:, :2 * c], 0.0).astype(BF16) for g in gmat]
    av = [_dot(ak, vs) for ak, vs in zip(a_k, v_st)]
    u = [l[:2 * c] + a[:2 * c] for l, a in zip(ls, av)]
    pw = a_ab
    n = 1
    while n < c:
        if 2 * n >= c:
            x = [_dot(p_, u_.astype(BF16)) for p_, u_ in zip(pw, u)]
            u = [u_ + x_ for u_, x_ in zip(u, x)]
        else:
            x = [_dot(p_, jnp.concatenate([u_.astype(BF16), p_], axis=1)) for p_, u_ in zip(pw, u)]
            u = [u_ + x_[:, :2 * c] for u_, x_ in zip(u, x)]
            pw = [x_[:, 2 * c:].astype(BF16) for x_ in x]
        n *= 2
    yst = [l[2 * c:] + a[2 * c:] + _dot(arb, u_.astype(BF16)) for l, a, arb, u_ in zip(ls, av, a_rb, u)]
    ys = [y_[:c] + y_[c:] for y_ in yst]
    s_new = []
    for p, u_, vs in zip(pairs, u, v_st):
        lhs_t = cat([u_, vs.astype(F32)]).T.astype(BF16)
        s_new.append(p[8] * p[7] + _dot(lhs_t, cat([st(p[5]), st(p[6])])))
    return ys, s_new


def _rwkv_kernel(p_ref, mu_ref, w0_ref, a0_ref, wch_ref, wcl_ref, guh_ref, gul_ref, kkp_ref, kap_ref,
                 rk_ref, gng_ref, gnb_ref, seg_ref, tri_ref, o_ref,
                 state_ref, carry_ref, al_s, rt_s, be_s, kh_s, v_s, bc_s, kc_s, gend_s, y_s, bonus_s,
                 gate_s, *, tb, nb):
    ti = pl.program_id(0)

    @pl.when(ti == 0)
    def _():
        state_ref[...] = jnp.zeros_like(state_ref)
        carry_ref[...] = jnp.zeros_like(carry_ref)

    p = p_ref[...].reshape(nb * tb, A_IN)
    row = lax.broadcasted_iota(jnp.int32, p.shape, 0)
    prev = pltpu.roll(p, 1, axis=0)
    for bi in range(nb):
        prev = jnp.where(row == bi * tb, carry_ref[bi:bi + 1, :], prev)
        carry_ref[bi:bi + 1, :] = p[(bi + 1) * tb - 1:(bi + 1) * tb, :]
    xs = p + (prev - p) * mu_ref[...]
    r = xs[:, 0:A_WIDTH]
    k = xs[:, A_WIDTH:2 * A_WIDTH]
    v = xs[:, 2 * A_WIDTH:3 * A_WIDTH]
    z = xs[:, 3 * A_WIDTH:3 * A_WIDTH + LANES]
    lane = lax.broadcasted_iota(jnp.int32, z.shape, 1)
    z = jnp.where(lane < A_DECAY_LORA, jnp.tanh(z), z)
    wa = _dot_hl(z, wch_ref[...], wcl_ref[...])
    w = -jax.nn.softplus(-(w0_ref[...] + wa[:, :A_WIDTH])) - 0.5
    lw = -jnp.exp(w)
    a = jax.nn.sigmoid(a0_ref[...] + wa[:, A_WIDTH:])
    gate = _dot_hl(jax.nn.sigmoid(xs[:, 3 * A_WIDTH + LANES:]), guh_ref[...], gul_ref[...])
    kk = k * kkp_ref[...]
    k = k * (1.0 + (a - 1.0) * kap_ref[...])
    seg = seg_ref[...]
    blocks = [slice(j * LANES, (j + 1) * LANES) for j in range(A_PAIRS)]
    segsum = lambda x: jnp.concatenate([_dot_x2(x[:, b], seg) for b in blocks], axis=1)
    kk = kk * lax.rsqrt(segsum(kk * kk) + 1e-12)
    gate_s[...] = gate
    bonus_s[...] = segsum(r * k * rk_ref[...]) * v * gate
    v_s[...] = v.astype(BF16)
    tri = tri_ref[...]
    for c in range(nb * tb // CHUNK):
        rows = slice(c * CHUNK, (c + 1) * CHUNK)
        lwc = lw[rows, :]
        cum = _dot_x3_left(tri, lwc)
        g = jnp.exp(cum)
        gi = jnp.exp(-cum)
        g_end = g[CHUNK - 1:CHUNK, :]
        be = kk[rows, :] * a[rows, :] * gi
        kh = k[rows, :] * gi
        al_s[rows, :] = (-kk[rows, :] * jnp.exp(cum - lwc)).astype(BF16)
        rt_s[rows, :] = (r[rows, :] * g).astype(BF16)
        be_s[rows, :] = be.astype(BF16)
        kh_s[rows, :] = kh.astype(BF16)
        bc_s[rows, :] = (be * g_end).astype(BF16)
        kc_s[rows, :] = (kh * g_end).astype(BF16)
        gend_s[8 * c:8 * c + 1, :] = g_end

    i2 = lax.broadcasted_iota(jnp.int32, (2 * CHUNK, 2 * CHUNK), 0)
    j2 = lax.broadcasted_iota(jnp.int32, (2 * CHUNK, 2 * CHUNK), 1)
    same = (i2 >> CHUNK_SHIFT) == (j2 >> CHUNK_SHIFT)
    strict = same & ((j2 & (CHUNK - 1)) < (i2 & (CHUNK - 1)))
    incl = same & ((j2 & (CHUNK - 1)) <= (i2 & (CHUNK - 1)))
    first = lax.broadcasted_iota(jnp.int32, (CHUNK, LANES), 1) < A_HEAD

    def chunk_body(c, carry):
        where = []
        for bi in range(nb):
            rows = pl.ds(pl.multiple_of(bi * tb + c * CHUNK, CHUNK), CHUNK)
            grow = pl.ds(pl.multiple_of((bi * (tb // CHUNK) + c) * 8, 8), 1)
            where += [(rows, grow, b, bi * A_PAIRS + j) for j, b in enumerate(blocks)]
        pairs = [(al_s[rows, b], rt_s[rows, b], be_s[rows, b], kh_s[rows, b], v_s[rows, b], bc_s[rows, b],
                  kc_s[rows, b], gend_s[grow, b], state_ref[sj]) for rows, grow, b, sj in where]
        ys, s_new = _rwkv_chunks(pairs, strict, incl, first)
        for (rows, _, b, sj), y_, s_ in zip(where, ys, s_new):
            y_s[rows, b] = y_
            state_ref[sj] = s_
        return carry

    lax.fori_loop(0, tb // CHUNK, chunk_body, 0)

    y = y_s[...]
    inv = 1.0 / A_HEAD
    mean = segsum(y) * inv
    yc = y - mean
    var = segsum(yc * yc) * inv
    y = yc * lax.rsqrt(var + RWKV_GN_EPS) * gng_ref[...] + gnb_ref[...]
    o_ref[...] = (y * gate_s[...] + bonus_s[...]).astype(o_ref.dtype).reshape(nb, tb, A_WIDTH)


def _rwkv(proj, prm, bsz, seq, tb):
    t = proj.shape[0]
    nt = seq // tb
    (mu, w0, a0, wch, wcl, guh, gul, kkp, kap, rk, gng, gnb) = prm
    seg = jnp.kron(jnp.eye(2, dtype=F32), jnp.ones((A_HEAD, A_HEAD), F32)).astype(BF16)
    tri = jnp.tril(jnp.ones((CHUNK, CHUNK), F32)).astype(BF16)
    full = lambda a_: pl.BlockSpec(a_.shape, lambda i: (0,) * a_.ndim)
    vm = lambda dt: pltpu.VMEM((bsz * tb, A_WIDTH), dt)
    assert bsz <= 8
    y = pl.pallas_call(
        functools.partial(_rwkv_kernel, tb=tb, nb=bsz),
        grid=(nt,),
        in_specs=[pl.BlockSpec((bsz, tb, A_IN), lambda i: (0, i, 0))]
        + [full(x) for x in (mu, w0, a0, wch, wcl, guh, gul, kkp, kap, rk, gng, gnb, seg, tri)],
        out_specs=pl.BlockSpec((bsz, tb, A_WIDTH), lambda i: (0, i, 0)),
        out_shape=jax.ShapeDtypeStruct((bsz, seq, A_WIDTH), BF16),
        scratch_shapes=[pltpu.VMEM((bsz * A_PAIRS, LANES, LANES), F32), pltpu.VMEM((8, A_IN), F32)]
        + [vm(BF16) for _ in range(7)] + [pltpu.VMEM((bsz * tb // CHUNK * 8, A_WIDTH), F32)]
        + [vm(F32) for _ in range(3)],
        compiler_params=pltpu.CompilerParams(
            dimension_semantics=("arbitrary",), vmem_limit_bytes=VMEM_LIMIT),
        name="rwkv7",
    )(proj.reshape(bsz, seq, proj.shape[1]), mu, w0, a0, wch, wcl, guh, gul, kkp, kap, rk, gng, gnb, seg, tri)
    return y.reshape(t, A_WIDTH)


def _rwkv_params(mu, w0, w_up, a0, a_up, g_up, k_k, k_a, r_k, gn_g, gn_b):
    row = lambda x: x.reshape(1, -1).astype(F32)
    zeros = jnp.zeros_like(w_up)
    wc = jnp.concatenate([jnp.concatenate([w_up, zeros], 1), jnp.concatenate([zeros, a_up], 1)], 0)
    hl = lambda m: (m.astype(BF16), (m - m.astype(BF16).astype(F32)).astype(BF16))
    wch, wcl = hl(wc)
    guh, gul = hl(g_up)
    return (row(mu), row(w0), row(a0), wch, wcl, guh, gul, row(k_k), row(k_a), row(r_k), row(gn_g),
            row(gn_b))


def _pick(n, pref):
    t = min(n, pref)
    while n % t:
        t //= 2
    return t


def kernel(x, norm_mix_g, w_in, rwkv_mu, rwkv_w0, rwkv_w_up, rwkv_a0, rwkv_a_up, rwkv_g_up, rwkv_k_k, rwkv_k_a, rwkv_r_k, rwkv_gn_g, rwkv_gn_b, s5_a_re, s5_a_im, s5_log_dt, s5_b_re, s5_b_im, s5_c_re, s5_c_im, s5_d, s5_w_glu, s5_b_glu, s5_beta, diff_lq1, diff_lk1, diff_lq2, diff_lk2, diff_subln_g, w_out, norm_ff_g, w_ff1, w_ff2, norm_final_g):
    bsz, seq, d = x.shape
    t = bsz * seq
    depth = w_in.shape[0]
    x2 = x.reshape(t, d).astype(F32)
    tm = _pick(t, 512)
    tm_in = _pick(t, 1024)
    tq = _pick(seq, 1024)
    tb = _pick(seq, 256)
    assert tq & (tq - 1) == 0 and (seq // S5_LC) & (seq // S5_LC - 1) == 0 and tb % CHUNK == 0
    rows_per_batch = seq // S5_LC
    n_levels = max(1, (rows_per_batch - 1).bit_length())
    for l in range(depth):
        proj, proj_c = _inproj(x2, norm_mix_g[l], w_in[l].astype(BF16), tm_in, 768, A_IN + B_WIDTH)
        prm = _rwkv_params(rwkv_mu[l], rwkv_w0[l], rwkv_w_up[l], rwkv_a0[l], rwkv_a_up[l], rwkv_g_up[l],
                           rwkv_k_k[l], rwkv_k_a[l], rwkv_r_k[l].reshape(-1), rwkv_gn_g[l], rwkv_gn_b[l])
        y_a = _rwkv(proj, prm, bsz, seq, tb)
        s5p = _s5_params(s5_a_re[l], s5_a_im[l], s5_log_dt[l], s5_b_re[l], s5_b_im[l], s5_c_re[l],
                         s5_c_im[l], s5_d[l], s5_w_glu[l], s5_b_glu[l], s5_beta[l], n_levels)
        y_b = _s5(proj, s5p, bsz, seq)
        lambda_init = 0.8 - 0.6 * math.exp(-0.3 * l)
        lam = (jnp.exp(jnp.sum(diff_lq1[l] * diff_lk1[l])) - jnp.exp(jnp.sum(diff_lq2[l] * diff_lk2[l]))
               + lambda_init)
        y_c = _attn(proj_c, lam, diff_subln_g[l], bsz, seq, lambda_init, tq)
        x2 = _outproj(x2, y_a, y_b, y_c, w_out[l].astype(BF16), tm)
        x2 = _ffn(x2, norm_ff_g[l], w_ff1[l].astype(BF16), w_ff2[l].astype(BF16), norm_final_g,
                  l == depth - 1, tm, 1024)
    return x2.reshape(bsz, seq, d)
```
